```python
import jax, jax.numpy as jnp
from jax import lax
import numpy as np

D_MODEL = 1024
BATCH = 16
SEQ = 2048
DEPTH = 1
DEC_BATCH = 128
DEC_SEQ = 1
PAST_LEN = 16384
PAGE_SIZE = 128

HEAD_DIM = 64
D_MIX = D_MODEL
N_Q_HEADS = 8
N_KV_HEADS = 2
Q_PER_KV = N_Q_HEADS // N_KV_HEADS
D_ATTN = N_Q_HEADS * HEAD_DIM
D_KV = N_KV_HEADS * HEAD_DIM
D_CONV = D_MIX - D_ATTN
WINDOW = 128
ATT_BLOCK = 128
CONV_WIDTH = 31
CONV_STATE = CONV_WIDTH - 1
D_IN = D_ATTN + 2 * D_KV + 2 * D_CONV
N_KEYS = 128
N_EXPERTS = N_KEYS * N_KEYS
PEER_HEADS = 8
PEER_TOPK = 16
D_KEY = 256
D_HALF = D_KEY // 2
PEER_CHUNK = 256
D_PLE = 256
EPS = 1e-6
NEG = -1e30

kernel_name = 'hymba_conformer_swa_peer_step'


def rmsnorm(x, g):
    xf = x.astype(jnp.float32)
    y = xf * lax.rsqrt(jnp.mean(xf * xf, axis=-1, keepdims=True) + EPS)
    return (y * g.astype(jnp.float32)).astype(x.dtype)


def layernorm(x, g, b):
    xf = x.astype(jnp.float32)
    mu = jnp.mean(xf, axis=-1, keepdims=True)
    xc = xf - mu
    y = xc * lax.rsqrt(jnp.mean(xc * xc, axis=-1, keepdims=True) + EPS)
    return (y * g.astype(jnp.float32) + b.astype(jnp.float32)).astype(x.dtype)


def alibi_slopes():
    s = np.array([2.0 ** (-8.0 * (h + 1) / N_Q_HEADS) for h in range(N_Q_HEADS)], np.float32)
    return jnp.asarray(s).reshape(N_KV_HEADS, Q_PER_KV, 1, 1)


def mixer_inputs(h, w_in, q_gain, k_gain):
    z = h @ w_in
    lead = h.shape[:-1]
    o1, o2, o3, o4 = D_ATTN, D_ATTN + D_KV, D_ATTN + 2 * D_KV, D_ATTN + 2 * D_KV + D_CONV
    q = rmsnorm(z[..., :o1].reshape(*lead, N_Q_HEADS, HEAD_DIM), q_gain)
    k = rmsnorm(z[..., o1:o2].reshape(*lead, N_KV_HEADS, HEAD_DIM), k_gain)
    v = z[..., o2:o3].reshape(*lead, N_KV_HEADS, HEAD_DIM)
    glu = z[..., o3:o4] * jax.nn.sigmoid(z[..., o4:])
    return q, k, v, glu


def sink_attention(q, k, v, sinks, dist, valid):
    s = jnp.einsum('...qhgd,...khd->...hgqk', q, k).astype(jnp.float32) * (HEAD_DIM ** -0.5)
    s = s - alibi_slopes() * dist.astype(jnp.float32)
    s = jnp.where(valid, s, NEG)
    sink = sinks.astype(jnp.float32).reshape(N_KV_HEADS, Q_PER_KV, 1, 1)
    m = jnp.maximum(jnp.max(s, axis=-1, keepdims=True), sink)
    e = jnp.exp(s - m)
    probs = e / (jnp.sum(e, axis=-1, keepdims=True) + jnp.exp(sink - m))
    return jnp.einsum('...hgqk,...khd->...qhgd', probs.astype(v.dtype), v)


def prompt_window_attention(q, k, v, sinks):
    B, S = q.shape[0], q.shape[1]
    nb = S // ATT_BLOCK
    qb = q.reshape(B, nb, ATT_BLOCK, N_KV_HEADS, Q_PER_KV, HEAD_DIM)
    kb = k.reshape(B, nb, ATT_BLOCK, N_KV_HEADS, HEAD_DIM)
    vb = v.reshape(B, nb, ATT_BLOCK, N_KV_HEADS, HEAD_DIM)
    prev = lambda t: jnp.concatenate([jnp.zeros_like(t[:, :1]), t[:, :-1]], axis=1)
    kk = jnp.concatenate([prev(kb), kb], axis=2)
    vv = jnp.concatenate([prev(vb), vb], axis=2)
    i = jnp.arange(ATT_BLOCK)[:, None]
    j = jnp.arange(2 * ATT_BLOCK)[None, :]
    dist = ATT_BLOCK + i - j
    band = (dist >= 0) & (dist <= WINDOW)
    has_prev = (jnp.arange(nb)[:, None, None] > 0) | (j >= ATT_BLOCK)[None]
    valid = (band[None] & has_prev)[:, None, None]
    out = sink_attention(qb, kk, vv, sinks, dist, valid)
    return out.reshape(B, S, D_ATTN)


def sample_window_attention(q, k, v, ck, cv, sinks):
    DB, T = q.shape[0], q.shape[1]
    kk = jnp.concatenate([ck, k], axis=1)
    vv = jnp.concatenate([cv, v], axis=1)
    qs = q.reshape(DB, T, N_KV_HEADS, Q_PER_KV, HEAD_DIM)
    i = jnp.arange(T)[:, None]
    j = jnp.arange(WINDOW + T)[None, :]
    dist = WINDOW + i - j
    valid = (dist >= 0) & (dist <= WINDOW)
    out = sink_attention(qs, kk, vv, sinks, dist, valid)
    return out.reshape(DB, T, D_ATTN), kk[:, -WINDOW:], vv[:, -WINDOW:]


def conv_branch(glu_hist, dw_w, dw_b, ln_g, ln_b):
    y = lax.conv_general_dilated(glu_hist, dw_w[:, None, :], window_strides=(1,), padding='VALID',
                                 dimension_numbers=('NWC', 'WIO', 'NWC'), feature_group_count=D_CONV)
    y = layernorm(y + dw_b, ln_g, ln_b)
    return jax.nn.silu(y)


def peer(h, w_pq, sub_keys, u_tab, v_tab):
    lead = h.shape[:-1]
    flat = h.reshape(-1, D_MODEL)
    n = flat.shape[0]
    pad = (-n) % PEER_CHUNK
    chunks = jnp.pad(flat, ((0, pad), (0, 0))).reshape(-1, PEER_CHUNK, D_MODEL)

    def one(xc):
        q = (xc @ w_pq).reshape(PEER_CHUNK, PEER_HEADS, 2, D_HALF)
        s = jnp.einsum('nhpd,pkd->nhpk', q, sub_keys).astype(jnp.float32)
        s1, i1 = lax.top_k(s[:, :, 0], PEER_TOPK)
        s2, i2 = lax.top_k(s[:, :, 1], PEER_TOPK)
        cand = (s1[..., :, None] + s2[..., None, :]).reshape(PEER_CHUNK, PEER_HEADS, PEER_TOPK * PEER_TOPK)
        cid = (i1[..., :, None] * N_KEYS + i2[..., None, :]).reshape(PEER_CHUNK, PEER_HEADS, PEER_TOPK * PEER_TOPK)
        top_s, pos = lax.top_k(cand, PEER_TOPK)
        eid = jnp.take_along_axis(cid, pos, axis=-1)
        g = jax.nn.softmax(top_s, axis=-1)
        act = jax.nn.gelu(jnp.einsum('nhkd,nd->nhk', u_tab[eid], xc).astype(jnp.float32), approximate=False)
        w = (g * act).astype(xc.dtype)
        return jnp.einsum('nhk,nhkd->nd', w, v_tab[eid])

    out = lax.map(one, chunks).reshape(-1, D_MODEL)[:n]
    return out.reshape(*lead, D_MODEL)


def finish(x, mixed, p, w_out, norm_ffn, peer_wq, peer_sub_keys, peer_u, peer_v, norm_ple, w_ple_proj, w_ple_gate):
    x = x + mixed @ w_out
    x = x + peer(rmsnorm(x, norm_ffn), peer_wq, peer_sub_keys, peer_u, peer_v)
    gate = jax.nn.sigmoid(rmsnorm(x, norm_ple) @ w_ple_gate)
    return x + (p @ w_ple_proj) * gate


def setup_inputs(seed: int = 0) -> dict:
    key = jax.random.key(seed)
    ks = jax.random.split(key, 26)
    nrm = lambda k, shape, s: s * jax.random.normal(k, shape, jnp.float32)
    L = DEPTH
    return {
        'x_prompt': nrm(ks[0], (BATCH, SEQ, D_MODEL), 1.0),
        'x_sample': nrm(ks[1], (DEC_BATCH, DEC_SEQ, D_MODEL), 1.0),
        'cache_k': nrm(ks[2], (L, DEC_BATCH, WINDOW, N_KV_HEADS, HEAD_DIM), 1.0),
        'cache_v': nrm(ks[3], (L, DEC_BATCH, WINDOW, N_KV_HEADS, HEAD_DIM), 1.0),
        'state_conv': nrm(ks[4], (L, DEC_BATCH, CONV_STATE, D_CONV), 0.5),
        'p_prompt': nrm(ks[5], (L, BATCH, SEQ, D_PLE), 1.0),
        'p_sample': nrm(ks[6], (L, DEC_BATCH, DEC_SEQ, D_PLE), 1.0),
        'norm_mix': 1.0 + nrm(ks[7], (L, D_MODEL), 0.01),
        'w_in': nrm(ks[8], (L, D_MODEL, D_IN), D_MODEL ** -0.5),
        'q_gain': 1.0 + nrm(ks[9], (L, HEAD_DIM), 0.01),
        'k_gain': 1.0 + nrm(ks[10], (L, HEAD_DIM), 0.01),
        'attn_sinks': nrm(ks[11], (L, N_Q_HEADS), 0.5),
        'conv_dw_w': nrm(ks[12], (L, CONV_WIDTH, D_CONV), CONV_WIDTH ** -0.5),
        'conv_dw_b': nrm(ks[13], (L, D_CONV), 0.01),
        'conv_ln_g': 1.0 + nrm(ks[14], (L, D_CONV), 0.01),
        'conv_ln_b': nrm(ks[15], (L, D_CONV), 0.01),
        'w_out': nrm(ks[16], (L, D_MIX, D_MODEL), D_MIX ** -0.5),
        'norm_ffn': 1.0 + nrm(ks[17], (L, D_MODEL), 0.01),
        'peer_wq': nrm(ks[18], (L, D_MODEL, PEER_HEADS * D_KEY), D_MODEL ** -0.5),
        'peer_sub_keys': nrm(ks[19], (L, 2, N_KEYS, D_HALF), D_HALF ** -0.5),
        'peer_u': nrm(ks[20], (L, N_EXPERTS, D_MODEL), D_MODEL ** -0.5),
        'peer_v': nrm(ks[21], (L, N_EXPERTS, D_MODEL), PEER_HEADS ** -0.5),
        'norm_ple': 1.0 + nrm(ks[22], (L, D_MODEL), 0.01),
        'w_ple_proj': nrm(ks[23], (L, D_PLE, D_MODEL), D_PLE ** -0.5),
        'w_ple_gate': nrm(ks[24], (L, D_MODEL, D_MODEL), D_MODEL ** -0.5),
    }


def reference(x_prompt, x_sample, cache_k, cache_v, state_conv, p_prompt, p_sample,
              norm_mix, w_in, q_gain, k_gain, attn_sinks, conv_dw_w, conv_dw_b, conv_ln_g, conv_ln_b,
              w_out, norm_ffn, peer_wq, peer_sub_keys, peer_u, peer_v, norm_ple, w_ple_proj, w_ple_gate):
    xp, xs = x_prompt, x_sample
    nk_p, nv_p, nc_p, nk_s, nv_s, nc_s = [], [], [], [], [], []
    for i in range(DEPTH):
        h = rmsnorm(xp, norm_mix[i])
        q, k, v, glu = mixer_inputs(h, w_in[i], q_gain[i], k_gain[i])
        att = prompt_window_attention(q, k, v, attn_sinks[i])
        conv_in = jnp.concatenate([jnp.zeros((glu.shape[0], CONV_STATE, D_CONV), glu.dtype), glu], axis=1)
        cvo = conv_branch(conv_in, conv_dw_w[i], conv_dw_b[i], conv_ln_g[i], conv_ln_b[i])
        mixed = jnp.concatenate([att, cvo], axis=-1)
        nk_p.append(k[:, -WINDOW:])
        nv_p.append(v[:, -WINDOW:])
        nc_p.append(conv_in[:, -CONV_STATE:])
        xp = finish(xp, mixed, p_prompt[i], w_out[i], norm_ffn[i], peer_wq[i], peer_sub_keys[i],
                    peer_u[i], peer_v[i], norm_ple[i], w_ple_proj[i], w_ple_gate[i])

        h = rmsnorm(xs, norm_mix[i])
        q, k, v, glu = mixer_inputs(h, w_in[i], q_gain[i], k_gain[i])
        att, kw, vw = sample_window_attention(q, k, v, cache_k[i], cache_v[i], attn_sinks[i])
        conv_in = jnp.concatenate([state_conv[i], glu], axis=1)
        cvo = conv_branch(conv_in, conv_dw_w[i], conv_dw_b[i], conv_ln_g[i], conv_ln_b[i])
        mixed = jnp.concatenate([att, cvo], axis=-1)
        nk_s.append(kw)
        nv_s.append(vw)
        nc_s.append(conv_in[:, -CONV_STATE:])
        xs = finish(xs, mixed, p_sample[i], w_out[i], norm_ffn[i], peer_wq[i], peer_sub_keys[i],
                    peer_u[i], peer_v[i], norm_ple[i], w_ple_proj[i], w_ple_gate[i])
    return (xp, xs, jnp.stack(nk_p), jnp.stack(nv_p), jnp.stack(nc_p), jnp.stack(nk_s), jnp.stack(nv_s), jnp.stack(nc_s))
```

```python
import functools
import math

import numpy as np
import jax
import jax.numpy as jnp
from jax import lax
from jax.experimental import pallas as pl
from jax.experimental.pallas import tpu as pltpu

HEAD_DIM = 64
N_Q_HEADS = 8
N_KV_HEADS = 2
Q_PER_KV = N_Q_HEADS // N_KV_HEADS
D_ATTN = N_Q_HEADS * HEAD_DIM
D_KV = N_KV_HEADS * HEAD_DIM
WINDOW = 128
ATT_BLOCK = 128
CONV_WIDTH = 31
CONV_STATE = CONV_WIDTH - 1
N_KEYS = 128
PEER_HEADS = 8
PEER_TOPK = 16
D_HALF = 128
EPS = 1e-6
NEG = -1e30
LANES = 128
SUBLANES = 8
CONV_PAD = 32
VMEM_LIMIT = 56 * 1024 * 1024

HEAD_ORDER = [0, 4, 1, 5, 2, 6, 3, 7]
SLOPES = [2.0 ** (-8.0 * (h + 1) / N_Q_HEADS) for h in range(N_Q_HEADS)]

CAND_AB = [(a, b) for a in range(PEER_TOPK) for b in range(PEER_TOPK // (a + 1))]
N_CAND = len(CAND_AB)
N_CAND_PAD = -(-N_CAND // 8) * 8


def _rms(x, g):
    return x * lax.rsqrt(jnp.mean(x * x, axis=-1, keepdims=True) + EPS) * g


def _head_rms(a, gain, lane_lo):
    cols = []
    for c in range(a.shape[1] // LANES):
        ac = a[:, c * LANES:(c + 1) * LANES]
        sq = ac * ac
        lo = jnp.sum(jnp.where(lane_lo, sq, 0.0), axis=-1, keepdims=True)
        hi = jnp.sum(jnp.where(lane_lo, 0.0, sq), axis=-1, keepdims=True)
        ms = jnp.where(lane_lo, lo, hi) * (1.0 / HEAD_DIM)
        cols.append(ac * lax.rsqrt(ms + EPS) * gain[:, c * LANES:(c + 1) * LANES])
    return cols


def _layernorm_silu(y, g, b):
    mu = jnp.mean(y, axis=-1, keepdims=True)
    yc = y - mu
    yn = yc * lax.rsqrt(jnp.mean(yc * yc, axis=-1, keepdims=True) + EPS) * g + b
    return yn * jax.nn.sigmoid(yn)


def _mixer_prompt_kernel(sinks_ref, x_ref, nmix_ref, win_ref, qg_ref, kg_ref, dww_ref, dwb_ref, lng_ref, lnb_ref,
                         wout_ref, x1_ref, nk_ref, nv_ref, nc_ref, kbuf, vbuf, gbuf, mixbuf):
    si = pl.program_id(1)
    T = x_ref.shape[1]
    d_conv = gbuf.shape[1]
    nblk = T // ATT_BLOCK

    @pl.when(si == 0)
    def _():
        kbuf[0:ATT_BLOCK, :] = jnp.zeros((ATT_BLOCK, D_KV), kbuf.dtype)
        vbuf[0:ATT_BLOCK, :] = jnp.zeros((ATT_BLOCK, D_KV), vbuf.dtype)
        gbuf[0:CONV_PAD, :] = jnp.zeros((CONV_PAD, d_conv), gbuf.dtype)

    x = x_ref[0]
    h = _rms(x, nmix_ref[...])
    z = jnp.dot(h.astype(jnp.bfloat16), win_ref[...], preferred_element_type=jnp.float32)
    o1, o2, o3, o4 = D_ATTN, D_ATTN + D_KV, D_ATTN + 2 * D_KV, D_ATTN + 2 * D_KV + d_conv
    lane_lo = lax.broadcasted_iota(jnp.int32, (T, LANES), 1) < HEAD_DIM

    qcols = _head_rms(z[:, :o1], qg_ref[...], lane_lo)
    kn = _head_rms(z[:, o1:o2], kg_ref[...], lane_lo)[0]
    v = z[:, o2:o3]
    glu = z[:, o3:o4] * jax.nn.sigmoid(z[:, o4:])

    kbuf[ATT_BLOCK:ATT_BLOCK + T, :] = kn.astype(kbuf.dtype)
    vbuf[ATT_BLOCK:ATT_BLOCK + T, :] = v.astype(vbuf.dtype)
    gbuf[CONV_PAD:CONV_PAD + T, :] = glu

    @pl.when(si == pl.num_programs(1) - 1)
    def _():
        nk_ref[0] = kn[T - WINDOW:, :]
        nv_ref[0] = v[T - WINDOW:, :]
        nc_ref[0] = glu[T - CONV_STATE:, :]

    ii = lax.broadcasted_iota(jnp.int32, (ATT_BLOCK, 2 * ATT_BLOCK), 0)
    jj = lax.broadcasted_iota(jnp.int32, (ATT_BLOCK, 2 * ATT_BLOCK), 1)
    dist = ATT_BLOCK + ii - jj
    band = (dist >= 0) & (dist <= WINDOW)
    distf = dist.astype(jnp.float32)
    lane_lo_b = lax.broadcasted_iota(jnp.int32, (ATT_BLOCK, LANES), 1) < HEAD_DIM
    scale = HEAD_DIM ** -0.5
    for blk in range(nblk):
        kk = kbuf[blk * ATT_BLOCK:(blk + 2) * ATT_BLOCK, :]
        vv = vbuf[blk * ATT_BLOCK:(blk + 2) * ATT_BLOCK, :]
        if blk == 0:
            valid = band & ((jj >= ATT_BLOCK) | (si > 0))
        else:
            valid = band
        for c in range(D_ATTN // LANES):
            qc = qcols[c][blk * ATT_BLOCK:(blk + 1) * ATT_BLOCK, :] * scale
            outs = []
            for e in range(2):
                head = HEAD_ORDER[2 * c + e]
                qe = jnp.where(lane_lo_b if e == 0 else ~lane_lo_b, qc, 0.0).astype(jnp.bfloat16)
                s = lax.dot_general(qe, kk, (((1,), (1,)), ((), ())), preferred_element_type=jnp.float32)
                s = s - SLOPES[head] * distf
                s = jnp.where(valid, s, NEG)
                sink = sinks_ref[head]
                m = jnp.maximum(jnp.max(s, axis=-1, keepdims=True), sink)
                ex = jnp.exp(s - m)
                den = jnp.sum(ex, axis=-1, keepdims=True) + jnp.exp(sink - m)
                p = ex / den
                outs.append(jnp.dot(p.astype(jnp.bfloat16), vv, preferred_element_type=jnp.float32))
            att = jnp.where(lane_lo_b, outs[0], outs[1])
            mixbuf[blk * ATT_BLOCK:(blk + 1) * ATT_BLOCK, c * LANES:(c + 1) * LANES] = att.astype(mixbuf.dtype)

    rows = 32
    dwb = dwb_ref[...]
    lng = lng_ref[...]
    lnb = lnb_ref[...]

    def conv_chunk(ci, carry):
        base = pl.multiple_of(ci * rows, rows)
        acc = jnp.zeros((rows, d_conv), jnp.float32)
        win = gbuf[pl.ds(base, rows + CONV_PAD), :]
        for w in range(CONV_WIDTH):
            lo = CONV_PAD - CONV_STATE + w
            acc = acc + win[lo:lo + rows, :] * dww_ref[w:w + 1, :]
        cv = _layernorm_silu(acc + dwb, lng, lnb)
        mixbuf[pl.ds(base, rows), D_ATTN:] = cv.astype(mixbuf.dtype)
        return carry

    lax.fori_loop(0, T // rows, conv_chunk, 0)

    x1_ref[0] = x + jnp.dot(mixbuf[...], wout_ref[...], preferred_element_type=jnp.float32)

    kbuf[0:ATT_BLOCK, :] = kbuf[T:T + ATT_BLOCK, :]
    vbuf[0:ATT_BLOCK, :] = vbuf[T:T + ATT_BLOCK, :]
    gbuf[0:CONV_PAD, :] = gbuf[T:T + CONV_PAD, :]


def _mixer_prompt(x, sinks, nmix, win, qg, kg, dww, dwb, lng, lnb, wout, tile):
    B, S, D = x.shape
    d_in = win.shape[1]
    d_conv = dww.shape[1]
    d_mix = wout.shape[0]
    full = lambda shape: pl.BlockSpec(shape, lambda b, s: (0,) * len(shape))
    return pl.pallas_call(
        _mixer_prompt_kernel,
        grid=(B, S // tile),
        in_specs=[
            pl.BlockSpec(memory_space=pltpu.SMEM),
            pl.BlockSpec((1, tile, D), lambda b, s: (b, s, 0)),
            full((1, D)), full((D, d_in)), full((1, D_ATTN)), full((1, D_KV)),
            full((CONV_WIDTH, d_conv)), full((1, d_conv)), full((1, d_conv)), full((1, d_conv)),
            full((d_mix, D)),
        ],
        out_specs=[
            pl.BlockSpec((1, tile, D), lambda b, s: (b, s, 0)),
            pl.BlockSpec((1, WINDOW, D_KV), lambda b, s: (b, 0, 0)),
            pl.BlockSpec((1, WINDOW, D_KV), lambda b, s: (b, 0, 0)),
            pl.BlockSpec((1, CONV_STATE, d_conv), lambda b, s: (b, 0, 0)),
        ],
        out_shape=[
            jax.ShapeDtypeStruct((B, S, D), jnp.float32),
            jax.ShapeDtypeStruct((B, WINDOW, D_KV), jnp.float32),
            jax.ShapeDtypeStruct((B, WINDOW, D_KV), jnp.float32),
            jax.ShapeDtypeStruct((B, CONV_STATE, d_conv), jnp.float32),
        ],
        scratch_shapes=[
            pltpu.VMEM((tile + ATT_BLOCK, D_KV), jnp.bfloat16),
            pltpu.VMEM((tile + ATT_BLOCK, D_KV), jnp.bfloat16),
            pltpu.VMEM((tile + CONV_PAD, d_conv), jnp.float32),
            pltpu.VMEM((tile, d_mix), jnp.bfloat16),
        ],
        compiler_params=pltpu.CompilerParams(
            dimension_semantics=("arbitrary", "arbitrary"), vmem_limit_bytes=VMEM_LIMIT),
        name="mixer_prompt",
    )(sinks, x, nmix, win, qg, kg, dww, dwb, lng, lnb, wout)


def _mixer_sample_kernel(sinks_ref, x_ref, ck_ref, cv_ref, st_ref, nmix_ref, win_ref, qg_ref, kg_ref, dww_ref,
                         dwb_ref, lng_ref, lnb_ref, wout_ref, x1_ref, nk_ref, nv_ref, nc_ref,
                         qbuf, knbuf, vnbuf, glubuf, mixbuf):
    nb = x_ref.shape[0]
    d_conv = st_ref.shape[2]
    x = x_ref[...]
    h = _rms(x, nmix_ref[...])
    z = jnp.dot(h.astype(jnp.bfloat16), win_ref[...], preferred_element_type=jnp.float32)
    o1, o2, o3, o4 = D_ATTN, D_ATTN + D_KV, D_ATTN + 2 * D_KV, D_ATTN + 2 * D_KV + d_conv
    lane_lo = lax.broadcasted_iota(jnp.int32, (nb, LANES), 1) < HEAD_DIM
    qcols = _head_rms(z[:, :o1], qg_ref[...], lane_lo)
    kn = _head_rms(z[:, o1:o2], kg_ref[...], lane_lo)[0]
    v = z[:, o2:o3]
    glu = z[:, o3:o4] * jax.nn.sigmoid(z[:, o4:])
    for c in range(D_ATTN // LANES):
        qbuf[:, c * LANES:(c + 1) * LANES] = qcols[c] * (HEAD_DIM ** -0.5)
    knbuf[...] = kn
    vnbuf[...] = v
    glubuf[...] = glu

    row_lo = lax.broadcasted_iota(jnp.int32, (1, LANES), 1) < HEAD_DIM
    hrow = lax.broadcasted_iota(jnp.int32, (N_Q_HEADS, LANES), 0)
    hlane = lax.broadcasted_iota(jnp.int32, (N_Q_HEADS, LANES), 1)
    keep = (hrow % 2) == (hlane // HEAD_DIM)
    jcol = lax.broadcasted_iota(jnp.int32, (N_Q_HEADS, WINDOW), 1)
    distf = (WINDOW - jcol).astype(jnp.float32)
    hcol = lax.broadcasted_iota(jnp.int32, (N_Q_HEADS, 1), 0)
    slope = jnp.zeros((N_Q_HEADS, 1), jnp.float32)
    sink = jnp.zeros((N_Q_HEADS, 1), jnp.float32)
    for r in range(N_Q_HEADS):
        slope = jnp.where(hcol == r, SLOPES[HEAD_ORDER[r]], slope)
        sink = jnp.where(hcol == r, sinks_ref[HEAD_ORDER[r]], sink)
    bias = slope * distf
    dw_hist = dww_ref[0:CONV_STATE, :]
    dw_last = dww_ref[CONV_STATE:CONV_WIDTH, :]

    def one_seq(b):
        qrow = qbuf[pl.ds(b, 1), :]
        qp = jnp.concatenate(
            [jnp.broadcast_to(qrow[:, c * LANES:(c + 1) * LANES], (2, LANES)) for c in range(D_ATTN // LANES)], axis=0)
        qp = jnp.where(keep, qp, 0.0)
        ckb = ck_ref[b]
        cvb = cv_ref[b]
        knr = knbuf[pl.ds(b, 1), :]
        vnr = vnbuf[pl.ds(b, 1), :]
        glur = glubuf[pl.ds(b, 1), :]
        stb = st_ref[b]

        nk_ref[b, 0:WINDOW - 1, :] = ckb[1:WINDOW, :]
        nk_ref[b, WINDOW - 1:WINDOW, :] = knr
        nv_ref[b, 0:WINDOW - 1, :] = cvb[1:WINDOW, :]
        nv_ref[b, WINDOW - 1:WINDOW, :] = vnr
        nc_ref[b, 0:CONV_STATE - 1, :] = stb[1:CONV_STATE, :]
        nc_ref[b, CONV_STATE - 1:CONV_STATE, :] = glur

        s = lax.dot_general(qp.astype(jnp.bfloat16), ckb.astype(jnp.bfloat16), (((1,), (1,)), ((), ())),
                            preferred_element_type=jnp.float32) - bias
        s_new = jnp.sum(qp * knr, axis=-1, keepdims=True)
        m = jnp.maximum(jnp.maximum(jnp.max(s, axis=-1, keepdims=True), s_new), sink)
        ex = jnp.exp(s - m)
        ex_new = jnp.exp(s_new - m)
        den = jnp.sum(ex, axis=-1, keepdims=True) + ex_new + jnp.exp(sink - m)
        o = jnp.dot((ex / den).astype(jnp.bfloat16), cvb.astype(jnp.bfloat16), preferred_element_type=jnp.float32)
        o = o + (ex_new / den) * vnr
        att = [jnp.where(row_lo, o[2 * c:2 * c + 1, :], o[2 * c + 1:2 * c + 2, :]) for c in range(D_ATTN // LANES)]
        conv = jnp.sum(stb * dw_hist, axis=0, keepdims=True) + glur * dw_last
        return jnp.concatenate(att + [conv], axis=1)

    def per_group(gi, carry):
        base = pl.multiple_of(gi * SUBLANES, SUBLANES)
        mixbuf[pl.ds(base, SUBLANES), :] = jnp.concatenate([one_seq(base + u) for u in range(SUBLANES)], axis=0)
        return carry

    lax.fori_loop(0, nb // SUBLANES, per_group, 0)

    cvo = _layernorm_silu(mixbuf[:, D_ATTN:] + dwb_ref[...], lng_ref[...], lnb_ref[...])
    mixbuf[:, D_ATTN:] = cvo
    x1_ref[...] = x + jnp.dot(mixbuf[...].astype(jnp.bfloat16), wout_ref[...], preferred_element_type=jnp.float32)


def _mixer_sample(x, ck, cv, st, sinks, nmix, win, qg, kg, dww, dwb, lng, lnb, wout, nb):
    DB, D = x.shape
    d_in = win.shape[1]
    d_conv = dww.shape[1]
    d_mix = wout.shape[0]
    full = lambda shape: pl.BlockSpec(shape, lambda i: (0,) * len(shape))
    return pl.pallas_call(
        _mixer_sample_kernel,
        grid=(DB // nb,),
        in_specs=[
            pl.BlockSpec(memory_space=pltpu.SMEM),
            pl.BlockSpec((nb, D), lambda i: (i, 0)),
            pl.BlockSpec((nb, WINDOW, D_KV), lambda i: (i, 0, 0)),
            pl.BlockSpec((nb, WINDOW, D_KV), lambda i: (i, 0, 0)),
            pl.BlockSpec((nb, CONV_STATE, d_conv), lambda i: (i, 0, 0)),
            full((1, D)), full((D, d_in)), full((1, D_ATTN)), full((1, D_KV)),
            full((CONV_WIDTH, d_conv)), full((1, d_conv)), full((1, d_conv)), full((1, d_conv)),
            full((d_mix, D)),
        ],
        out_specs=[
            pl.BlockSpec((nb, D), lambda i: (i, 0)),
            pl.BlockSpec((nb, WINDOW, D_KV), lambda i: (i, 0, 0)),
            pl.BlockSpec((nb, WINDOW, D_KV), lambda i: (i, 0, 0)),
            pl.BlockSpec((nb, CONV_STATE, d_conv), lambda i: (i, 0, 0)),
        ],
        out_shape=[
            jax.ShapeDtypeStruct((DB, D), jnp.float32),
            jax.ShapeDtypeStruct((DB, WINDOW, D_KV), jnp.float32),
            jax.ShapeDtypeStruct((DB, WINDOW, D_KV), jnp.float32),
            jax.ShapeDtypeStruct((DB, CONV_STATE, d_conv), jnp.float32),
        ],
        scratch_shapes=[
            pltpu.VMEM((nb, D_ATTN), jnp.float32),
            pltpu.VMEM((nb, D_KV), jnp.float32),
            pltpu.VMEM((nb, D_KV), jnp.float32),
            pltpu.VMEM((nb, d_conv), jnp.float32),
            pltpu.VMEM((nb, d_mix), jnp.float32),
        ],
        compiler_params=pltpu.CompilerParams(dimension_semantics=("arbitrary",), vmem_limit_bytes=VMEM_LIMIT),
        name="mixer_sample",
    )(sinks, x, ck, cv, st, nmix, win, qg, kg, dww, dwb, lng, lnb, wout)


def _top16_rows(s, row_id, big):
    vals, ids = [], []
    for _ in range(PEER_TOPK):
        m = jnp.max(s, axis=0, keepdims=True)
        im = jnp.min(jnp.where(s == m, row_id, big), axis=0, keepdims=True)
        vals.append(m)
        ids.append(im)
        s = jnp.where(row_id == im, -jnp.inf, s)
    return jnp.concatenate(vals, axis=0), jnp.concatenate(ids, axis=0)


def _route_kernel(x1_ref, nffn_ref, wqT_ref, sk_ref, cpos_ref, eid_ref, g_ref, qT):
    T = x1_ref.shape[0]
    nchunk = T // LANES
    h2 = _rms(x1_ref[...], nffn_ref[...])
    qT[...] = lax.dot_general(wqT_ref[...], h2.astype(jnp.bfloat16), (((1,), (1,)), ((), ())),
                              preferred_element_type=jnp.float32).astype(qT.dtype)
    key_id = lax.broadcasted_iota(jnp.int32, (N_KEYS, LANES), 0)
    cpos = cpos_ref[...]
    neg_pad = jnp.full((N_CAND_PAD - N_CAND, LANES), -jnp.inf, jnp.float32)
    zero_pad = jnp.zeros((N_CAND_PAD - N_CAND, LANES), jnp.int32)

    def body(i, carry):
        hh = i // nchunk
        c = i % nchunk
        rows = pl.ds(pl.multiple_of(hh * 2 * D_HALF, 2 * D_HALF), 2 * D_HALF)
        cols = pl.ds(pl.multiple_of(c * LANES, LANES), LANES)
        qb = qT[rows, cols]
        sa = jnp.dot(sk_ref[0], qb[:D_HALF], preferred_element_type=jnp.float32)
        sb = jnp.dot(sk_ref[1], qb[D_HALF:], preferred_element_type=jnp.float32)
        va, ia = _top16_rows(sa, key_id, N_KEYS)
        vb, ib = _top16_rows(sb, key_id, N_KEYS)
        cand, cid = [], []
        for a in range(PEER_TOPK):
            nbb = PEER_TOPK // (a + 1)
            cand.append(va[a:a + 1] + vb[0:nbb])
            cid.append(ia[a:a + 1] * N_KEYS + ib[0:nbb])
        cand = jnp.concatenate(cand + [neg_pad], axis=0)
        cid = jnp.concatenate(cid + [zero_pad], axis=0)
        tops, eids = [], []
        for _ in range(PEER_TOPK):
            m = jnp.max(cand, axis=0, keepdims=True)
            pm = jnp.min(jnp.where(cand == m, cpos, PEER_TOPK * PEER_TOPK), axis=0, keepdims=True)
            hit = cpos == pm
            tops.append(m)
            eids.append(jnp.max(jnp.where(hit, cid, -1), axis=0, keepdims=True))
            cand = jnp.where(hit, -jnp.inf, cand)
        ts = jnp.concatenate(tops, axis=0)
        ex = jnp.exp(ts - ts[0:1])
        g = ex / jnp.sum(ex, axis=0, keepdims=True)
        orow = pl.ds(pl.multiple_of(hh * PEER_TOPK, PEER_TOPK), PEER_TOPK)
        eid_ref[orow, cols] = jnp.concatenate(eids, axis=0)
        g_ref[orow, cols] = g
        return carry

    lax.fori_loop(0, PEER_HEADS * nchunk, body, 0)


def _route(x1, nffn, wqT, sk, cpos, tile):
    N, D = x1.shape
    nq = wqT.shape[0]
    full = lambda shape: pl.BlockSpec(shape, lambda i: (0,) * len(shape))
    rows = PEER_HEADS * PEER_TOPK
    return pl.pallas_call(
        _route_kernel,
        grid=(N // tile,),
        in_specs=[
            pl.BlockSpec((tile, D), lambda i: (i, 0)),
            full((1, D)), full((nq, D)), full((2, N_KEYS, D_HALF)), full((N_CAND_PAD, LANES)),
        ],
        out_specs=[
            pl.BlockSpec((rows, tile), lambda i: (0, i)),
            pl.BlockSpec((rows, tile), lambda i: (0, i)),
        ],
        out_shape=[
            jax.ShapeDtypeStruct((rows, N), jnp.int32),
            jax.ShapeDtypeStruct((rows, N), jnp.float32),
        ],
        scratch_shapes=[pltpu.VMEM((nq, tile), jnp.bfloat16)],
        compiler_params=pltpu.CompilerParams(dimension_semantics=("arbitrary",), vmem_limit_bytes=VMEM_LIMIT),
        name="peer_route",
    )(x1, nffn, wqT, sk, cpos)


N_SLOTS = 4


def _experts_kernel(eid_ref, g_ref, x1_ref, nffn_ref, tab_ref, out_ref, buf, h2buf, sem):
    T = x1_ref.shape[0]
    D = x1_ref.shape[1]
    rows = PEER_HEADS * PEER_TOPK
    h2buf[...] = _rms(x1_ref[...], nffn_ref[...])

    def row_copy(t, slot, j):
        return pltpu.make_async_copy(tab_ref.at[eid_ref[j, t]], buf.at[slot, j], sem.at[slot])

    def issue(t, slot):
        for j in range(rows):
            row_copy(t, slot, j).start()

    def wait_all(slot):
        pltpu.make_async_copy(tab_ref.at[pl.ds(0, rows)], buf.at[slot], sem.at[slot]).wait()

    for t0 in range(N_SLOTS - 1):
        issue(t0, t0)

    lane_id = lax.broadcasted_iota(jnp.int32, (rows, T), 1)
    inv_sqrt2 = 1.0 / math.sqrt(2.0)

    def one_token(t, u):
        slot = u % N_SLOTS
        nxt = t + (N_SLOTS - 1)

        @pl.when(nxt < T)
        def _():
            issue(nxt, (u + N_SLOTS - 1) % N_SLOTS)

        wait_all(slot)
        hrow = h2buf[pl.ds(t, 1), :]
        sc = jnp.sum(buf[slot, :, 0:D] * hrow, axis=1, keepdims=True)
        act = 0.5 * sc * (1.0 + lax.erf(sc * inv_sqrt2))
        gcol = jnp.sum(jnp.where(lane_id == t, g_ref[...], 0.0), axis=1, keepdims=True)
        w = gcol * act
        return jnp.sum(buf[slot, :, D:2 * D] * w, axis=0, keepdims=True)

    def body(gi, carry):
        base = pl.multiple_of(gi * SUBLANES, SUBLANES)
        out_ref[pl.ds(base, SUBLANES), :] = jnp.concatenate(
            [one_token(base + u, u) for u in range(SUBLANES)], axis=0)
        return carry

    lax.fori_loop(0, T // SUBLANES, body, 0)


def _experts(eidT, gT, x1, nffn, tab, tile):
    N, D = x1.shape
    rows = PEER_HEADS * PEER_TOPK
    return pl.pallas_call(
        _experts_kernel,
        grid=(N // tile,),
        in_specs=[
            pl.BlockSpec((rows, tile), lambda i: (0, i), memory_space=pltpu.SMEM),
            pl.BlockSpec((rows, tile), lambda i: (0, i)),
            pl.BlockSpec((tile, D), lambda i: (i, 0)),
            pl.BlockSpec((1, D), lambda i: (0, 0)),
            pl.BlockSpec(memory_space=pl.ANY),
        ],
        out_specs=pl.BlockSpec((tile, D), lambda i: (i, 0)),
        out_shape=jax.ShapeDtypeStruct((N, D), jnp.float32),
        scratch_shapes=[
            pltpu.VMEM((N_SLOTS, rows, 2 * D), jnp.float32),
            pltpu.VMEM((tile, D), jnp.float32),
            pltpu.SemaphoreType.DMA((N_SLOTS,)),
        ],
        compiler_params=pltpu.CompilerParams(dimension_semantics=("arbitrary",), vmem_limit_bytes=VMEM_LIMIT),
        name="peer_experts",
    )(eidT, gT, x1, nffn, tab)


def _finish_kernel(x1_ref, peer_ref, p_ref, nple_ref, wgate_ref, wproj_ref, y_ref):
    x2 = x1_ref[...] + peer_ref[...]
    hp = _rms(x2, nple_ref[...])
    gate = jax.nn.sigmoid(jnp.dot(hp.astype(jnp.bfloat16), wgate_ref[...], preferred_element_type=jnp.float32))
    proj = jnp.dot(p_ref[...].astype(jnp.bfloat16), wproj_ref[...], preferred_element_type=jnp.float32)
    y_ref[...] = x2 + proj * gate


def _finish(x1, peer, p, nple, wgate, wproj, tile):
    N, D = x1.shape
    dp = p.shape[1]
    full = lambda shape: pl.BlockSpec(shape, lambda i: (0,) * len(shape))
    return pl.pallas_call(
        _finish_kernel,
        grid=(N // tile,),
        in_specs=[
            pl.BlockSpec((tile, D), lambda i: (i, 0)),
            pl.BlockSpec((tile, D), lambda i: (i, 0)),
            pl.BlockSpec((tile, dp), lambda i: (i, 0)),
            full((1, D)), full((D, D)), full((dp, D)),
        ],
        out_specs=pl.BlockSpec((tile, D), lambda i: (i, 0)),
        out_shape=jax.ShapeDtypeStruct((N, D), jnp.float32),
        compiler_params=pltpu.CompilerParams(dimension_semantics=("arbitrary",), vmem_limit_bytes=VMEM_LIMIT),
        name="finish",
    )(x1, peer, p, nple, wgate, wproj)


def _tile_for(n, pref):
    t = min(pref, n)
    while n % t:
        t //= 2
    return t


def kernel(x_prompt, x_sample, cache_k, cache_v, state_conv, p_prompt, p_sample, norm_mix, w_in, q_gain, k_gain,
           attn_sinks, conv_dw_w, conv_dw_b, conv_ln_g, conv_ln_b, w_out, norm_ffn, peer_wq, peer_sub_keys, peer_u,
           peer_v, norm_ple, w_ple_proj, w_ple_gate):
    depth = norm_mix.shape[0]
    assert depth == 1
    B, S, D = x_prompt.shape
    DB, dec_seq, _ = x_sample.shape
    assert dec_seq == 1
    bf = jnp.bfloat16
    i = 0

    qperm = np.concatenate([np.arange(h * HEAD_DIM, (h + 1) * HEAD_DIM) for h in HEAD_ORDER])
    win = jnp.concatenate([w_in[i][:, qperm], w_in[i][:, D_ATTN:]], axis=1).astype(bf)
    wout = jnp.concatenate([w_out[i][qperm], w_out[i][D_ATTN:]], axis=0).astype(bf)
    qg = jnp.tile(q_gain[i], N_Q_HEADS)[None]
    kg = jnp.tile(k_gain[i], N_KV_HEADS)[None]
    sinks = attn_sinks[i]
    nmix = norm_mix[i][None]
    dww, dwb, lng, lnb = conv_dw_w[i], conv_dw_b[i][None], conv_ln_g[i][None], conv_ln_b[i][None]
    nffn = norm_ffn[i][None]
    wqT = peer_wq[i].T.astype(bf)
    sk = peer_sub_keys[i].astype(bf)
    tab = jnp.concatenate([peer_u[i], peer_v[i]], axis=1)
    nple = norm_ple[i][None]
    wgate = w_ple_gate[i].astype(bf)
    wproj = w_ple_proj[i].astype(bf)
    cpos = np.full((N_CAND_PAD, LANES), PEER_TOPK * PEER_TOPK, np.int32)
    for r, (a, b) in enumerate(CAND_AB):
        cpos[r, :] = a * PEER_TOPK + b
    cpos = jnp.asarray(cpos)

    def tail(x1, p):
        n = x1.shape[0]
        eidT, gT = _route(x1, nffn, wqT, sk, cpos, _tile_for(n, 512))
        peer = _experts(eidT, gT, x1, nffn, tab, _tile_for(n, 128))
        return _finish(x1, peer, p, nple, wgate, wproj, _tile_for(n, 512))

    x1p, nkp, nvp, ncp = _mixer_prompt(x_prompt, sinks, nmix, win, qg, kg, dww, dwb, lng, lnb, wout, _tile_for(S, 512))
    yp = tail(x1p.reshape(B * S, D), p_prompt[i].reshape(B * S, -1)).reshape(B, S, D)

    ck = cache_k[i].reshape(DB, WINDOW, D_KV)
    cv = cache_v[i].reshape(DB, WINDOW, D_KV)
    x1s, nks, nvs, ncs = _mixer_sample(x_sample.reshape(DB, D), ck, cv, state_conv[i], sinks, nmix, win, qg, kg,
                                       dww, dwb, lng, lnb, wout, _tile_for(DB, 32))
    ys = tail(x1s, p_sample[i].reshape(DB, -1)).reshape(DB, 1, D)

    kv5 = lambda a: a.reshape(1, a.shape[0], WINDOW, N_KV_HEADS, HEAD_DIM)
    return (yp, ys, kv5(nkp), kv5(nvp), ncp[None], kv5(nks), kv5(nvs), ncs[None])
```

```python
import functools
import math

import numpy as np
import jax
import jax.numpy as jnp
from jax import lax
from jax.experimental import pallas as pl
from jax.experimental.pallas import tpu as pltpu

HEAD_DIM = 64
N_Q_HEADS = 8
N_KV_HEADS = 2
Q_PER_KV = N_Q_HEADS // N_KV_HEADS
D_ATTN = N_Q_HEADS * HEAD_DIM
D_KV = N_KV_HEADS * HEAD_DIM
WINDOW = 128
ATT_BLOCK = 128
CONV_WIDTH = 31
CONV_STATE = CONV_WIDTH - 1
N_KEYS = 128
PEER_HEADS = 8
PEER_TOPK = 16
D_HALF = 128
EPS = 1e-6
NEG = -1e30
LANES = 128
SUBLANES = 8
CONV_PAD = 32
VMEM_LIMIT = 56 * 1024 * 1024

HEAD_ORDER = [0, 4, 1, 5, 2, 6, 3, 7]
SLOPES = [2.0 ** (-8.0 * (h + 1) / N_Q_HEADS) for h in range(N_Q_HEADS)]

CAND_AB = [(a, b) for a in range(PEER_TOPK) for b in range(PEER_TOPK // (a + 1))]
N_CAND = len(CAND_AB)
N_CAND_PAD = -(-N_CAND // 8) * 8


def _rms(x, g):
    return x * lax.rsqrt(jnp.mean(x * x, axis=-1, keepdims=True) + EPS) * g


def _head_rms(a, gain, lane_lo):
    cols = []
    for c in range(a.shape[1] // LANES):
        ac = a[:, c * LANES:(c + 1) * LANES]
        sq = ac * ac
        lo = jnp.sum(jnp.where(lane_lo, sq, 0.0), axis=-1, keepdims=True)
        hi = jnp.sum(jnp.where(lane_lo, 0.0, sq), axis=-1, keepdims=True)
        ms = jnp.where(lane_lo, lo, hi) * (1.0 / HEAD_DIM)
        cols.append(ac * lax.rsqrt(ms + EPS) * gain[:, c * LANES:(c + 1) * LANES])
    return cols


def _layernorm_silu(y, g, b):
    mu = jnp.mean(y, axis=-1, keepdims=True)
    yc = y - mu
    yn = yc * lax.rsqrt(jnp.mean(yc * yc, axis=-1, keepdims=True) + EPS) * g + b
    return yn * jax.nn.sigmoid(yn)


def _mixer_prompt_kernel(sinks_ref, x_ref, nmix_ref, win_ref, qg_ref, kg_ref, dww_ref, dwb_ref, lng_ref, lnb_ref,
                         wout_ref, x1_ref, nk_ref, nv_ref, nc_ref, kbuf, vbuf, gbuf, mixbuf):
    si = pl.program_id(1)
    T = x_ref.shape[1]
    d_conv = gbuf.shape[1]
    nblk = T // ATT_BLOCK

    @pl.when(si == 0)
    def _():
        kbuf[0:ATT_BLOCK, :] = jnp.zeros((ATT_BLOCK, D_KV), kbuf.dtype)
        vbuf[0:ATT_BLOCK, :] = jnp.zeros((ATT_BLOCK, D_KV), vbuf.dtype)
        gbuf[0:CONV_PAD, :] = jnp.zeros((CONV_PAD, d_conv), gbuf.dtype)

    x = x_ref[0]
    h = _rms(x, nmix_ref[...])
    z = jnp.dot(h.astype(jnp.bfloat16), win_ref[...], preferred_element_type=jnp.float32)
    o1, o2, o3, o4 = D_ATTN, D_ATTN + D_KV, D_ATTN + 2 * D_KV, D_ATTN + 2 * D_KV + d_conv
    lane_lo = lax.broadcasted_iota(jnp.int32, (T, LANES), 1) < HEAD_DIM

    qcols = _head_rms(z[:, :o1], qg_ref[...], lane_lo)
    kn = _head_rms(z[:, o1:o2], kg_ref[...], lane_lo)[0]
    v = z[:, o2:o3]
    glu = z[:, o3:o4] * jax.nn.sigmoid(z[:, o4:])

    kbuf[ATT_BLOCK:ATT_BLOCK + T, :] = kn.astype(kbuf.dtype)
    vbuf[ATT_BLOCK:ATT_BLOCK + T, :] = v.astype(vbuf.dtype)
    gbuf[CONV_PAD:CONV_PAD + T, :] = glu

    @pl.when(si == pl.num_programs(1) - 1)
    def _():
        nk_ref[0] = kn[T - WINDOW:, :]
        nv_ref[0] = v[T - WINDOW:, :]
        nc_ref[0] = glu[T - CONV_STATE:, :]

    ii = lax.broadcasted_iota(jnp.int32, (ATT_BLOCK, 2 * ATT_BLOCK), 0)
    jj = lax.broadcasted_iota(jnp.int32, (ATT_BLOCK, 2 * ATT_BLOCK), 1)
    dist = ATT_BLOCK + ii - jj
    band = (dist >= 0) & (dist <= WINDOW)
    distf = dist.astype(jnp.float32)
    lane_lo_b = lax.broadcasted_iota(jnp.int32, (ATT_BLOCK, LANES), 1) < HEAD_DIM
    scale = HEAD_DIM ** -0.5
    for blk in range(nblk):
        kk = kbuf[blk * ATT_BLOCK:(blk + 2) * ATT_BLOCK, :]
        vv = vbuf[blk * ATT_BLOCK:(blk + 2) * ATT_BLOCK, :]
        if blk == 0:
            valid = band & ((jj >= ATT_BLOCK) | (si > 0))
        else:
            valid = band
        for c in range(D_ATTN // LANES):
            qc = qcols[c][blk * ATT_BLOCK:(blk + 1) * ATT_BLOCK, :] * scale
            outs = []
            for e in range(2):
                head = HEAD_ORDER[2 * c + e]
                qe = jnp.where(lane_lo_b if e == 0 else ~lane_lo_b, qc, 0.0).astype(jnp.bfloat16)
                s = lax.dot_general(qe, kk, (((1,), (1,)), ((), ())), preferred_element_type=jnp.float32)
                s = s - SLOPES[head] * distf
                s = jnp.where(valid, s, NEG)
                sink = sinks_ref[head]
                m = jnp.maximum(jnp.max(s, axis=-1, keepdims=True), sink)
                ex = jnp.exp(s - m)
                den = jnp.sum(ex, axis=-1, keepdims=True) + jnp.exp(sink - m)
                p = ex / den
                outs.append(jnp.dot(p.astype(jnp.bfloat16), vv, preferred_element_type=jnp.float32))
            att = jnp.where(lane_lo_b, outs[0], outs[1])
            mixbuf[blk * ATT_BLOCK:(blk + 1) * ATT_BLOCK, c * LANES:(c + 1) * LANES] = att.astype(mixbuf.dtype)

    rows = 32
    dwb = dwb_ref[...]
    lng = lng_ref[...]
    lnb = lnb_ref[...]

    def conv_chunk(ci, carry):
        base = pl.multiple_of(ci * rows, rows)
        acc = jnp.zeros((rows, d_conv), jnp.float32)
        win = gbuf[pl.ds(base, rows + CONV_PAD), :]
        for w in range(CONV_WIDTH):
            lo = CONV_PAD - CONV_STATE + w
            acc = acc + win[lo:lo + rows, :] * dww_ref[w:w + 1, :]
        cv = _layernorm_silu(acc + dwb, lng, lnb)
        mixbuf[pl.ds(base, rows), D_ATTN:] = cv.astype(mixbuf.dtype)
        return carry

    lax.fori_loop(0, T // rows, conv_chunk, 0)

    x1_ref[0] = x + jnp.dot(mixbuf[...], wout_ref[...], preferred_element_type=jnp.float32)

    kbuf[0:ATT_BLOCK, :] = kbuf[T:T + ATT_BLOCK, :]
    vbuf[0:ATT_BLOCK, :] = vbuf[T:T + ATT_BLOCK, :]
    gbuf[0:CONV_PAD, :] = gbuf[T:T + CONV_PAD, :]


def _mixer_prompt(x, sinks, nmix, win, qg, kg, dww, dwb, lng, lnb, wout, tile):
    B, S, D = x.shape
    d_in = win.shape[1]
    d_conv = dww.shape[1]
    d_mix = wout.shape[0]
    full = lambda shape: pl.BlockSpec(shape, lambda b, s: (0,) * len(shape))
    return pl.pallas_call(
        _mixer_prompt_kernel,
        grid=(B, S // tile),
        in_specs=[
            pl.BlockSpec(memory_space=pltpu.SMEM),
            pl.BlockSpec((1, tile, D), lambda b, s: (b, s, 0)),
            full((1, D)), full((D, d_in)), full((1, D_ATTN)), full((1, D_KV)),
            full((CONV_WIDTH, d_conv)), full((1, d_conv)), full((1, d_conv)), full((1, d_conv)),
            full((d_mix, D)),
        ],
        out_specs=[
            pl.BlockSpec((1, tile, D), lambda b, s: (b, s, 0)),
            pl.BlockSpec((1, WINDOW, D_KV), lambda b, s: (b, 0, 0)),
            pl.BlockSpec((1, WINDOW, D_KV), lambda b, s: (b, 0, 0)),
            pl.BlockSpec((1, CONV_STATE, d_conv), lambda b, s: (b, 0, 0)),
        ],
        out_shape=[
            jax.ShapeDtypeStruct((B, S, D), jnp.float32),
            jax.ShapeDtypeStruct((B, WINDOW, D_KV), jnp.float32),
            jax.ShapeDtypeStruct((B, WINDOW, D_KV), jnp.float32),
            jax.ShapeDtypeStruct((B, CONV_STATE, d_conv), jnp.float32),
        ],
        scratch_shapes=[
            pltpu.VMEM((tile + ATT_BLOCK, D_KV), jnp.bfloat16),
            pltpu.VMEM((tile + ATT_BLOCK, D_KV), jnp.bfloat16),
            pltpu.VMEM((tile + CONV_PAD, d_conv), jnp.float32),
            pltpu.VMEM((tile, d_mix), jnp.bfloat16),
        ],
        compiler_params=pltpu.CompilerParams(
            dimension_semantics=("arbitrary", "arbitrary"), vmem_limit_bytes=VMEM_LIMIT),
        name="mixer_prompt",
    )(sinks, x, nmix, win, qg, kg, dww, dwb, lng, lnb, wout)


def _mixer_sample_kernel(sinks_ref, x_ref, ck_ref, cv_ref, st_ref, nmix_ref, win_ref, qg_ref, kg_ref, dww_ref,
                         dwb_ref, lng_ref, lnb_ref, wout_ref, x1_ref, nk_ref, nv_ref, nc_ref,
                         qbuf, knbuf, vnbuf, glubuf, mixbuf):
    nb = x_ref.shape[0]
    d_conv = st_ref.shape[2]
    x = x_ref[...]
    h = _rms(x, nmix_ref[...])
    z = jnp.dot(h.astype(jnp.bfloat16), win_ref[...], preferred_element_type=jnp.float32)
    o1, o2, o3, o4 = D_ATTN, D_ATTN + D_KV, D_ATTN + 2 * D_KV, D_ATTN + 2 * D_KV + d_conv
    lane_lo = lax.broadcasted_iota(jnp.int32, (nb, LANES), 1) < HEAD_DIM
    qcols = _head_rms(z[:, :o1], qg_ref[...], lane_lo)
    kn = _head_rms(z[:, o1:o2], kg_ref[...], lane_lo)[0]
    v = z[:, o2:o3]
    glu = z[:, o3:o4] * jax.nn.sigmoid(z[:, o4:])
    for c in range(D_ATTN // LANES):
        qbuf[:, c * LANES:(c + 1) * LANES] = qcols[c] * (HEAD_DIM ** -0.5)
    knbuf[...] = kn
    vnbuf[...] = v
    glubuf[...] = glu

    row_lo = lax.broadcasted_iota(jnp.int32, (1, LANES), 1) < HEAD_DIM
    hrow = lax.broadcasted_iota(jnp.int32, (N_Q_HEADS, LANES), 0)
    hlane = lax.broadcasted_iota(jnp.int32, (N_Q_HEADS, LANES), 1)
    keep = (hrow % 2) == (hlane // HEAD_DIM)
    jcol = lax.broadcasted_iota(jnp.int32, (N_Q_HEADS, WINDOW), 1)
    distf = (WINDOW - jcol).astype(jnp.float32)
    hcol = lax.broadcasted_iota(jnp.int32, (N_Q_HEADS, 1), 0)
    slope = jnp.zeros((N_Q_HEADS, 1), jnp.float32)
    sink = jnp.zeros((N_Q_HEADS, 1), jnp.float32)
    for r in range(N_Q_HEADS):
        slope = jnp.where(hcol == r, SLOPES[HEAD_ORDER[r]], slope)
        sink = jnp.where(hcol == r, sinks_ref[HEAD_ORDER[r]], sink)
    bias = slope * distf
    dw_hist = dww_ref[0:CONV_STATE, :]
    dw_last = dww_ref[CONV_STATE:CONV_WIDTH, :]

    def one_seq(b):
        qrow = qbuf[pl.ds(b, 1), :]
        qp = jnp.concatenate(
            [jnp.broadcast_to(qrow[:, c * LANES:(c + 1) * LANES], (2, LANES)) for c in range(D_ATTN // LANES)], axis=0)
        qp = jnp.where(keep, qp, 0.0)
        ckb = ck_ref[b]
        cvb = cv_ref[b]
        knr = knbuf[pl.ds(b, 1), :]
        vnr = vnbuf[pl.ds(b, 1), :]
        glur = glubuf[pl.ds(b, 1), :]
        stb = st_ref[b]

        nk_ref[b, 0:WINDOW - 1, :] = ckb[1:WINDOW, :]
        nk_ref[b, WINDOW - 1:WINDOW, :] = knr
        nv_ref[b, 0:WINDOW - 1, :] = cvb[1:WINDOW, :]
        nv_ref[b, WINDOW - 1:WINDOW, :] = vnr
        nc_ref[b, 0:CONV_STATE - 1, :] = stb[1:CONV_STATE, :]
        nc_ref[b, CONV_STATE - 1:CONV_STATE, :] = glur

        s = lax.dot_general(qp.astype(jnp.bfloat16), ckb.astype(jnp.bfloat16), (((1,), (1,)), ((), ())),
                            preferred_element_type=jnp.float32) - bias
        s_new = jnp.sum(qp * knr, axis=-1, keepdims=True)
        m = jnp.maximum(jnp.maximum(jnp.max(s, axis=-1, keepdims=True), s_new), sink)
        ex = jnp.exp(s - m)
        ex_new = jnp.exp(s_new - m)
        den = jnp.sum(ex, axis=-1, keepdims=True) + ex_new + jnp.exp(sink - m)
        o = jnp.dot((ex / den).astype(jnp.bfloat16), cvb.astype(jnp.bfloat16), preferred_element_type=jnp.float32)
        o = o + (ex_new / den) * vnr
        att = [jnp.where(row_lo, o[2 * c:2 * c + 1, :], o[2 * c + 1:2 * c + 2, :]) for c in range(D_ATTN // LANES)]
        conv = jnp.sum(stb * dw_hist, axis=0, keepdims=True) + glur * dw_last
        return jnp.concatenate(att + [conv], axis=1)

    def per_group(gi, carry):
        base = pl.multiple_of(gi * SUBLANES, SUBLANES)
        mixbuf[pl.ds(base, SUBLANES), :] = jnp.concatenate([one_seq(base + u) for u in range(SUBLANES)], axis=0)
        return carry

    lax.fori_loop(0, nb // SUBLANES, per_group, 0)

    cvo = _layernorm_silu(mixbuf[:, D_ATTN:] + dwb_ref[...], lng_ref[...], lnb_ref[...])
    mixbuf[:, D_ATTN:] = cvo
    x1_ref[...] = x + jnp.dot(mixbuf[...].astype(jnp.bfloat16), wout_ref[...], preferred_element_type=jnp.float32)


def _mixer_sample(x, ck, cv, st, sinks, nmix, win, qg, kg, dww, dwb, lng, lnb, wout, nb):
    DB, D = x.shape
    d_in = win.shape[1]
    d_conv = dww.shape[1]
    d_mix = wout.shape[0]
    full = lambda shape: pl.BlockSpec(shape, lambda i: (0,) * len(shape))
    return pl.pallas_call(
        _mixer_sample_kernel,
        grid=(DB // nb,),
        in_specs=[
            pl.BlockSpec(memory_space=pltpu.SMEM),
            pl.BlockSpec((nb, D), lambda i: (i, 0)),
            pl.BlockSpec((nb, WINDOW, D_KV), lambda i: (i, 0, 0)),
            pl.BlockSpec((nb, WINDOW, D_KV), lambda i: (i, 0, 0)),
            pl.BlockSpec((nb, CONV_STATE, d_conv), lambda i: (i, 0, 0)),
            full((1, D)), full((D, d_in)), full((1, D_ATTN)), full((1, D_KV)),
            full((CONV_WIDTH, d_conv)), full((1, d_conv)), full((1, d_conv)), full((1, d_conv)),
            full((d_mix, D)),
        ],
        out_specs=[
            pl.BlockSpec((nb, D), lambda i: (i, 0)),
            pl.BlockSpec((nb, WINDOW, D_KV), lambda i: (i, 0, 0)),
            pl.BlockSpec((nb, WINDOW, D_KV), lambda i: (i, 0, 0)),
            pl.BlockSpec((nb, CONV_STATE, d_conv), lambda i: (i, 0, 0)),
        ],
        out_shape=[
            jax.ShapeDtypeStruct((DB, D), jnp.float32),
            jax.ShapeDtypeStruct((DB, WINDOW, D_KV), jnp.float32),
            jax.ShapeDtypeStruct((DB, WINDOW, D_KV), jnp.float32),
            jax.ShapeDtypeStruct((DB, CONV_STATE, d_conv), jnp.float32),
        ],
        scratch_shapes=[
            pltpu.VMEM((nb, D_ATTN), jnp.float32),
            pltpu.VMEM((nb, D_KV), jnp.float32),
            pltpu.VMEM((nb, D_KV), jnp.float32),
            pltpu.VMEM((nb, d_conv), jnp.float32),
            pltpu.VMEM((nb, d_mix), jnp.float32),
        ],
        compiler_params=pltpu.CompilerParams(dimension_semantics=("arbitrary",), vmem_limit_bytes=VMEM_LIMIT),
        name="mixer_sample",
    )(sinks, x, ck, cv, st, nmix, win, qg, kg, dww, dwb, lng, lnb, wout)


def _top16_rows(s, row_id, big):
    vals, ids = [], []
    for _ in range(PEER_TOPK):
        m = jnp.max(s, axis=0, keepdims=True)
        im = jnp.min(jnp.where(s == m, row_id, big), axis=0, keepdims=True)
        vals.append(m)
        ids.append(im)
        s = jnp.where(row_id == im, -jnp.inf, s)
    return jnp.concatenate(vals, axis=0), jnp.concatenate(ids, axis=0)


def _route_kernel(x1_ref, nffn_ref, wqT_ref, sk_ref, cpos_ref, eid_ref, g_ref, qT):
    T = x1_ref.shape[0]
    nchunk = T // LANES
    h2 = _rms(x1_ref[...], nffn_ref[...])
    qT[...] = lax.dot_general(wqT_ref[...], h2.astype(jnp.bfloat16), (((1,), (1,)), ((), ())),
                              preferred_element_type=jnp.float32).astype(qT.dtype)
    key_id = lax.broadcasted_iota(jnp.int32, (N_KEYS, LANES), 0)
    cpos = cpos_ref[...]
    neg_pad = jnp.full((N_CAND_PAD - N_CAND, LANES), -jnp.inf, jnp.float32)
    zero_pad = jnp.zeros((N_CAND_PAD - N_CAND, LANES), jnp.int32)

    def body(i, carry):
        hh = i // nchunk
        c = i % nchunk
        rows = pl.ds(pl.multiple_of(hh * 2 * D_HALF, 2 * D_HALF), 2 * D_HALF)
        cols = pl.ds(pl.multiple_of(c * LANES, LANES), LANES)
        qb = qT[rows, cols]
        sa = jnp.dot(sk_ref[0], qb[:D_HALF], preferred_element_type=jnp.float32)
        sb = jnp.dot(sk_ref[1], qb[D_HALF:], preferred_element_type=jnp.float32)
        va, ia = _top16_rows(sa, key_id, N_KEYS)
        vb, ib = _top16_rows(sb, key_id, N_KEYS)
        cand, cid = [], []
        for a in range(PEER_TOPK):
            nbb = PEER_TOPK // (a + 1)
            cand.append(va[a:a + 1] + vb[0:nbb])
            cid.append(ia[a:a + 1] * N_KEYS + ib[0:nbb])
        cand = jnp.concatenate(cand + [neg_pad], axis=0)
        cid = jnp.concatenate(cid + [zero_pad], axis=0)
        tops, eids = [], []
        for _ in range(PEER_TOPK):
            m = jnp.max(cand, axis=0, keepdims=True)
            pm = jnp.min(jnp.where(cand == m, cpos, PEER_TOPK * PEER_TOPK), axis=0, keepdims=True)
            hit = cpos == pm
            tops.append(m)
            eids.append(jnp.max(jnp.where(hit, cid, -1), axis=0, keepdims=True))
            cand = jnp.where(hit, -jnp.inf, cand)
        ts = jnp.concatenate(tops, axis=0)
        ex = jnp.exp(ts - ts[0:1])
        g = ex / jnp.sum(ex, axis=0, keepdims=True)
        orow = pl.ds(pl.multiple_of(hh * PEER_TOPK, PEER_TOPK), PEER_TOPK)
        eid_ref[orow, cols] = jnp.concatenate(eids, axis=0)
        g_ref[orow, cols] = g
        return carry

    lax.fori_loop(0, PEER_HEADS * nchunk, body, 0)


def _route(x1, nffn, wqT, sk, cpos, tile):
    N, D = x1.shape
    nq = wqT.shape[0]
    full = lambda shape: pl.BlockSpec(shape, lambda i: (0,) * len(shape))
    rows = PEER_HEADS * PEER_TOPK
    return pl.pallas_call(
        _route_kernel,
        grid=(N // tile,),
        in_specs=[
            pl.BlockSpec((tile, D), lambda i: (i, 0)),
            full((1, D)), full((nq, D)), full((2, N_KEYS, D_HALF)), full((N_CAND_PAD, LANES)),
        ],
        out_specs=[
            pl.BlockSpec((rows, tile), lambda i: (0, i)),
            pl.BlockSpec((rows, tile), lambda i: (0, i)),
        ],
        out_shape=[
            jax.ShapeDtypeStruct((rows, N), jnp.int32),
            jax.ShapeDtypeStruct((rows, N), jnp.float32),
        ],
        scratch_shapes=[pltpu.VMEM((nq, tile), jnp.bfloat16)],
        compiler_params=pltpu.CompilerParams(dimension_semantics=("arbitrary",), vmem_limit_bytes=VMEM_LIMIT),
        name="peer_route",
    )(x1, nffn, wqT, sk, cpos)


GROUP = SUBLANES
N_SLOTS = 2 * GROUP
BITREV = [0, 4, 2, 6, 1, 5, 3, 7]


def _sublane_sums(ps, sub):
    def merge(a, b, keep_a, shift):
        x = jnp.where(keep_a, a, pltpu.roll(b, shift, 0))
        y = jnp.where(keep_a, pltpu.roll(a, SUBLANES - shift, 0), b)
        return x + y

    z = [merge(ps[2 * i], ps[2 * i + 1], sub < 4, 4) for i in range(4)]
    w = [merge(z[2 * i], z[2 * i + 1], (sub % 4) < 2, 2) for i in range(2)]
    return merge(w[0], w[1], (sub % 2) < 1, 1)


def _experts_kernel(eid_ref, eid_next_ref, g_ref, x1_ref, nffn_ref, tab_ref, out_ref, buf, h2buf, sem):
    step = pl.program_id(0)
    T = x1_ref.shape[0]
    n_groups = T // GROUP
    rows = PEER_HEADS * PEER_TOPK
    d = x1_ref.shape[1] * x1_ref.shape[2]

    x = x1_ref[...]
    ms = jnp.sum(jnp.sum(x * x, axis=2, keepdims=True), axis=1, keepdims=True) * (1.0 / d)
    h2buf[...] = x * lax.rsqrt(ms + EPS) * nffn_ref[...]

    def issue_token(ids_ref, col, slot):
        for j in range(rows):
            dst_row = (j // SUBLANES) * SUBLANES + BITREV[j % SUBLANES]
            pltpu.make_async_copy(tab_ref.at[ids_ref[j, col]], buf.at[slot, dst_row], sem.at[slot]).start()

    def wait_token(slot):
        pltpu.make_async_copy(tab_ref.at[pl.ds(0, rows)], buf.at[slot], sem.at[slot]).wait()

    lane_id = lax.broadcasted_iota(jnp.int32, (rows, T), 1)
    sub = lax.broadcasted_iota(jnp.int32, (SUBLANES, LANES), 0)
    inv_sqrt2 = 1.0 / math.sqrt(2.0)

    def compute_token(t, slot):
        h = h2buf[t]
        gcol = jnp.sum(jnp.where(lane_id == t, g_ref[...], 0.0), axis=1, keepdims=True)
        acc = [jnp.zeros((SUBLANES, LANES), jnp.float32) for _ in range(4)]
        for jg in range(rows // SUBLANES):
            r0 = jg * SUBLANES
            ps = [buf[slot, r0 + k, 0:SUBLANES, :] * h for k in range(SUBLANES)]
            sc = jnp.sum(_sublane_sums(ps, sub), axis=1, keepdims=True)
            act = 0.5 * sc * (1.0 + lax.erf(sc * inv_sqrt2))
            w = gcol[r0:r0 + SUBLANES] * act
            for m in range(SUBLANES):
                v = buf[slot, r0 + BITREV[m], SUBLANES:2 * SUBLANES, :]
                acc[m % 4] = acc[m % 4] + v * w[m:m + 1, :]
        out_ref[t] = (acc[0] + acc[1]) + (acc[2] + acc[3])

    def run_group(gi, par, issue_next):
        for u in range(GROUP):
            if issue_next is not None:
                issue_next(u, (1 - par) * GROUP + u)
            slot = par * GROUP + u
            wait_token(slot)
            compute_token(gi * GROUP + u, slot)

    @pl.when(step == 0)
    def _():
        for u in range(GROUP):
            issue_token(eid_ref, u, u)

    def body(gi, carry):
        run_group(gi, gi % 2, lambda u, slot: issue_token(eid_ref, (gi + 1) * GROUP + u, slot))
        return carry

    lax.fori_loop(0, n_groups - 1, body, 0)

    last_par = (n_groups - 1) % 2

    @pl.when(step + 1 < pl.num_programs(0))
    def _():
        for u in range(GROUP):
            issue_token(eid_next_ref, u, (1 - last_par) * GROUP + u)

    run_group(n_groups - 1, last_par, None)


def _experts(eidT, gT, x1, nffn, tab, tile):
    N, D = x1.shape
    rows = PEER_HEADS * PEER_TOPK
    chunks = D // LANES
    assert chunks == SUBLANES and (tile // GROUP) % 2 == 0
    nsteps = N // tile
    out = pl.pallas_call(
        _experts_kernel,
        grid=(nsteps,),
        in_specs=[
            pl.BlockSpec((rows, tile), lambda i: (0, i), memory_space=pltpu.SMEM),
            pl.BlockSpec((rows, tile), lambda i: (0, jnp.minimum(i + 1, nsteps - 1)), memory_space=pltpu.SMEM),
            pl.BlockSpec((rows, tile), lambda i: (0, i)),
            pl.BlockSpec((tile, chunks, LANES), lambda i: (i, 0, 0)),
            pl.BlockSpec((chunks, LANES), lambda i: (0, 0)),
            pl.BlockSpec(memory_space=pl.ANY),
        ],
        out_specs=pl.BlockSpec((tile, chunks, LANES), lambda i: (i, 0, 0)),
        out_shape=jax.ShapeDtypeStruct((N, chunks, LANES), jnp.float32),
        scratch_shapes=[
            pltpu.VMEM((N_SLOTS, rows, 2 * chunks, LANES), jnp.float32),
            pltpu.VMEM((tile, chunks, LANES), jnp.float32),
            pltpu.SemaphoreType.DMA((N_SLOTS,)),
        ],
        compiler_params=pltpu.CompilerParams(dimension_semantics=("arbitrary",), vmem_limit_bytes=VMEM_LIMIT),
        name="peer_experts",
    )(eidT, eidT, gT, x1.reshape(N, chunks, LANES), nffn.reshape(chunks, LANES), tab)
    return out.reshape(N, D)


def _finish_kernel(x1_ref, peer_ref, p_ref, nple_ref, wgate_ref, wproj_ref, y_ref):
    x2 = x1_ref[...] + peer_ref[...]
    hp = _rms(x2, nple_ref[...])
    gate = jax.nn.sigmoid(jnp.dot(hp.astype(jnp.bfloat16), wgate_ref[...], preferred_element_type=jnp.float32))
    proj = jnp.dot(p_ref[...].astype(jnp.bfloat16), wproj_ref[...], preferred_element_type=jnp.float32)
    y_ref[...] = x2 + proj * gate


def _finish(x1, peer, p, nple, wgate, wproj, tile):
    N, D = x1.shape
    dp = p.shape[1]
    full = lambda shape: pl.BlockSpec(shape, lambda i: (0,) * len(shape))
    return pl.pallas_call(
        _finish_kernel,
        grid=(N // tile,),
        in_specs=[
            pl.BlockSpec((tile, D), lambda i: (i, 0)),
            pl.BlockSpec((tile, D), lambda i: (i, 0)),
            pl.BlockSpec((tile, dp), lambda i: (i, 0)),
            full((1, D)), full((D, D)), full((dp, D)),
        ],
        out_specs=pl.BlockSpec((tile, D), lambda i: (i, 0)),
        out_shape=jax.ShapeDtypeStruct((N, D), jnp.float32),
        compiler_params=pltpu.CompilerParams(dimension_semantics=("arbitrary",), vmem_limit_bytes=VMEM_LIMIT),
        name="finish",
    )(x1, peer, p, nple, wgate, wproj)


def _tile_for(n, pref):
    t = min(pref, n)
    while n % t:
        t //= 2
    return t


def kernel(x_prompt, x_sample, cache_k, cache_v, state_conv, p_prompt, p_sample, norm_mix, w_in, q_gain, k_gain,
           attn_sinks, conv_dw_w, conv_dw_b, conv_ln_g, conv_ln_b, w_out, norm_ffn, peer_wq, peer_sub_keys, peer_u,
           peer_v, norm_ple, w_ple_proj, w_ple_gate):
    depth = norm_mix.shape[0]
    assert depth == 1
    B, S, D = x_prompt.shape
    DB, dec_seq, _ = x_sample.shape
    assert dec_seq == 1
    bf = jnp.bfloat16
    i = 0

    qperm = np.concatenate([np.arange(h * HEAD_DIM, (h + 1) * HEAD_DIM) for h in HEAD_ORDER])
    win = jnp.concatenate([w_in[i][:, qperm], w_in[i][:, D_ATTN:]], axis=1).astype(bf)
    wout = jnp.concatenate([w_out[i][qperm], w_out[i][D_ATTN:]], axis=0).astype(bf)
    qg = jnp.tile(q_gain[i], N_Q_HEADS)[None]
    kg = jnp.tile(k_gain[i], N_KV_HEADS)[None]
    sinks = attn_sinks[i]
    nmix = norm_mix[i][None]
    dww, dwb, lng, lnb = conv_dw_w[i], conv_dw_b[i][None], conv_ln_g[i][None], conv_ln_b[i][None]
    nffn = norm_ffn[i][None]
    wqT = peer_wq[i].T.astype(bf)
    sk = peer_sub_keys[i].astype(bf)
    n_exp = peer_u.shape[1]
    tab = jnp.concatenate([peer_u[i].reshape(n_exp, D // LANES, LANES), peer_v[i].reshape(n_exp, D // LANES, LANES)],
                          axis=1)
    nple = norm_ple[i][None]
    wgate = w_ple_gate[i].astype(bf)
    wproj = w_ple_proj[i].astype(bf)
    cpos = np.full((N_CAND_PAD, LANES), PEER_TOPK * PEER_TOPK, np.int32)
    for r, (a, b) in enumerate(CAND_AB):
        cpos[r, :] = a * PEER_TOPK + b
    cpos = jnp.asarray(cpos)

    def tail(x1, p):
        n = x1.shape[0]
        eidT, gT = _route(x1, nffn, wqT, sk, cpos, _tile_for(n, 512))
        peer = _experts(eidT, gT, x1, nffn, tab, _tile_for(n, 128))
        return _finish(x1, peer, p, nple, wgate, wproj, _tile_for(n, 512))

    x1p, nkp, nvp, ncp = _mixer_prompt(x_prompt, sinks, nmix, win, qg, kg, dww, dwb, lng, lnb, wout, _tile_for(S, 512))
    yp = tail(x1p.reshape(B * S, D), p_prompt[i].reshape(B * S, -1)).reshape(B, S, D)

    ck = cache_k[i].reshape(DB, WINDOW, D_KV)
    cv = cache_v[i].reshape(DB, WINDOW, D_KV)
    x1s, nks, nvs, ncs = _mixer_sample(x_sample.reshape(DB, D), ck, cv, state_conv[i], sinks, nmix, win, qg, kg,
                                       dww, dwb, lng, lnb, wout, _tile_for(DB, 32))
    ys = tail(x1s, p_sample[i].reshape(DB, -1)).reshape(DB, 1, D)

    kv5 = lambda a: a.reshape(1, a.shape[0], WINDOW, N_KV_HEADS, HEAD_DIM)
    return (yp, ys, kv5(nkp), kv5(nvp), ncp[None], kv5(nks), kv5(nvs), ncs[None])
```

```python
import functools
import math

import numpy as np
import jax
import jax.numpy as jnp
from jax import lax
from jax.experimental import pallas as pl
from jax.experimental.pallas import tpu as pltpu

HEAD_DIM = 64
N_Q_HEADS = 8
N_KV_HEADS = 2
Q_PER_KV = N_Q_HEADS // N_KV_HEADS
D_ATTN = N_Q_HEADS * HEAD_DIM
D_KV = N_KV_HEADS * HEAD_DIM
WINDOW = 128
ATT_BLOCK = 128
CONV_WIDTH = 31
CONV_STATE = CONV_WIDTH - 1
N_KEYS = 128
PEER_HEADS = 8
PEER_TOPK = 16
D_HALF = 128
EPS = 1e-6
NEG = -1e30
LANES = 128
SUBLANES = 8
CONV_PAD = 32
VMEM_LIMIT = 56 * 1024 * 1024

HEAD_ORDER = [0, 4, 1, 5, 2, 6, 3, 7]
SLOPES = [2.0 ** (-8.0 * (h + 1) / N_Q_HEADS) for h in range(N_Q_HEADS)]

CAND_AB = [(a, b) for a in range(PEER_TOPK) for b in range(PEER_TOPK // (a + 1))]
N_CAND = len(CAND_AB)
N_CAND_PAD = -(-N_CAND // 8) * 8


def _rms(x, g):
    return x * lax.rsqrt(jnp.mean(x * x, axis=-1, keepdims=True) + EPS) * g


def _head_rms(a, gain, lane_lo):
    cols = []
    for c in range(a.shape[1] // LANES):
        ac = a[:, c * LANES:(c + 1) * LANES]
        sq = ac * ac
        lo = jnp.sum(jnp.where(lane_lo, sq, 0.0), axis=-1, keepdims=True)
        hi = jnp.sum(jnp.where(lane_lo, 0.0, sq), axis=-1, keepdims=True)
        ms = jnp.where(lane_lo, lo, hi) * (1.0 / HEAD_DIM)
        cols.append(ac * lax.rsqrt(ms + EPS) * gain[:, c * LANES:(c + 1) * LANES])
    return cols


def _layernorm_silu(y, g, b):
    mu = jnp.mean(y, axis=-1, keepdims=True)
    yc = y - mu
    yn = yc * lax.rsqrt(jnp.mean(yc * yc, axis=-1, keepdims=True) + EPS) * g + b
    return yn * jax.nn.sigmoid(yn)


def _mixer_prompt_kernel(sinks_ref, x_ref, nmix_ref, win_ref, qg_ref, kg_ref, dww_ref, dwb_ref, lng_ref, lnb_ref,
                         wout_ref, x1_ref, nk_ref, nv_ref, nc_ref, kbuf, vbuf, gbuf, mixbuf):
    si = pl.program_id(1)
    T = x_ref.shape[1]
    d_conv = gbuf.shape[1]
    nblk = T // ATT_BLOCK

    @pl.when(si == 0)
    def _():
        kbuf[0:ATT_BLOCK, :] = jnp.zeros((ATT_BLOCK, D_KV), kbuf.dtype)
        vbuf[0:ATT_BLOCK, :] = jnp.zeros((ATT_BLOCK, D_KV), vbuf.dtype)
        gbuf[0:CONV_PAD, :] = jnp.zeros((CONV_PAD, d_conv), gbuf.dtype)

    x = x_ref[0]
    h = _rms(x, nmix_ref[...])
    z = jnp.dot(h.astype(jnp.bfloat16), win_ref[...], preferred_element_type=jnp.float32)
    o1, o2, o3, o4 = D_ATTN, D_ATTN + D_KV, D_ATTN + 2 * D_KV, D_ATTN + 2 * D_KV + d_conv
    lane_lo = lax.broadcasted_iota(jnp.int32, (T, LANES), 1) < HEAD_DIM

    qcols = _head_rms(z[:, :o1], qg_ref[...], lane_lo)
    kn = _head_rms(z[:, o1:o2], kg_ref[...], lane_lo)[0]
    v = z[:, o2:o3]
    glu = z[:, o3:o4] * jax.nn.sigmoid(z[:, o4:])

    kbuf[ATT_BLOCK:ATT_BLOCK + T, :] = kn.astype(kbuf.dtype)
    vbuf[ATT_BLOCK:ATT_BLOCK + T, :] = v.astype(vbuf.dtype)
    gbuf[CONV_PAD:CONV_PAD + T, :] = glu

    @pl.when(si == pl.num_programs(1) - 1)
    def _():
        nk_ref[0] = kn[T - WINDOW:, :]
        nv_ref[0] = v[T - WINDOW:, :]
        nc_ref[0] = glu[T - CONV_STATE:, :]

    rows = 32
    dwb = dwb_ref[...]
    lng = lng_ref[...]
    lnb = lnb_ref[...]
    first = CONV_PAD - CONV_STATE

    def conv_chunk(base):
        acc = jnp.zeros((rows, d_conv), jnp.float32)
        win = gbuf[base:base + rows + CONV_PAD, :]
        for r in range(SUBLANES):
            span = (rows + CONV_PAD - r) // SUBLANES * SUBLANES
            shifted = win[r:r + span, :]
            for w in range(CONV_WIDTH):
                if (first + w) % SUBLANES == r:
                    k = first + w - r
                    acc = acc + shifted[k:k + rows, :] * dww_ref[w:w + 1, :]
        cv = _layernorm_silu(acc + dwb, lng, lnb)
        mixbuf[base:base + rows, D_ATTN:] = cv.astype(mixbuf.dtype)

    ii = lax.broadcasted_iota(jnp.int32, (ATT_BLOCK, 2 * ATT_BLOCK), 0)
    jj = lax.broadcasted_iota(jnp.int32, (ATT_BLOCK, 2 * ATT_BLOCK), 1)
    dist = ATT_BLOCK + ii - jj
    band = (dist >= 0) & (dist <= WINDOW)
    distf = dist.astype(jnp.float32)
    lane_lo_b = lax.broadcasted_iota(jnp.int32, (ATT_BLOCK, LANES), 1) < HEAD_DIM
    scale = HEAD_DIM ** -0.5
    for blk in range(nblk):
        kk = kbuf[blk * ATT_BLOCK:(blk + 2) * ATT_BLOCK, :]
        vv = vbuf[blk * ATT_BLOCK:(blk + 2) * ATT_BLOCK, :]
        if blk == 0:
            valid = band & ((jj >= ATT_BLOCK) | (si > 0))
        else:
            valid = band
        for c in range(D_ATTN // LANES):
            qc = qcols[c][blk * ATT_BLOCK:(blk + 1) * ATT_BLOCK, :] * scale
            outs = []
            for e in range(2):
                head = HEAD_ORDER[2 * c + e]
                qe = jnp.where(lane_lo_b if e == 0 else ~lane_lo_b, qc, 0.0).astype(jnp.bfloat16)
                s = lax.dot_general(qe, kk, (((1,), (1,)), ((), ())), preferred_element_type=jnp.float32)
                s = s - SLOPES[head] * distf
                s = jnp.where(valid, s, NEG)
                sink = sinks_ref[head]
                m = jnp.maximum(jnp.max(s, axis=-1, keepdims=True), sink)
                ex = jnp.exp(s - m)
                den = jnp.sum(ex, axis=-1, keepdims=True) + jnp.exp(sink - m)
                p = ex / den
                outs.append(jnp.dot(p.astype(jnp.bfloat16), vv, preferred_element_type=jnp.float32))
            att = jnp.where(lane_lo_b, outs[0], outs[1])
            mixbuf[blk * ATT_BLOCK:(blk + 1) * ATT_BLOCK, c * LANES:(c + 1) * LANES] = att.astype(mixbuf.dtype)
        for cc in range(ATT_BLOCK // rows):
            conv_chunk(blk * ATT_BLOCK + cc * rows)

    x1_ref[0] = x + jnp.dot(mixbuf[...], wout_ref[...], preferred_element_type=jnp.float32)

    kbuf[0:ATT_BLOCK, :] = kbuf[T:T + ATT_BLOCK, :]
    vbuf[0:ATT_BLOCK, :] = vbuf[T:T + ATT_BLOCK, :]
    gbuf[0:CONV_PAD, :] = gbuf[T:T + CONV_PAD, :]


def _mixer_prompt(x, sinks, nmix, win, qg, kg, dww, dwb, lng, lnb, wout, tile):
    B, S, D = x.shape
    d_in = win.shape[1]
    d_conv = dww.shape[1]
    d_mix = wout.shape[0]
    full = lambda shape: pl.BlockSpec(shape, lambda b, s: (0,) * len(shape))
    return pl.pallas_call(
        _mixer_prompt_kernel,
        grid=(B, S // tile),
        in_specs=[
            pl.BlockSpec(memory_space=pltpu.SMEM),
            pl.BlockSpec((1, tile, D), lambda b, s: (b, s, 0)),
            full((1, D)), full((D, d_in)), full((1, D_ATTN)), full((1, D_KV)),
            full((CONV_WIDTH, d_conv)), full((1, d_conv)), full((1, d_conv)), full((1, d_conv)),
            full((d_mix, D)),
        ],
        out_specs=[
            pl.BlockSpec((1, tile, D), lambda b, s: (b, s, 0)),
            pl.BlockSpec((1, WINDOW, D_KV), lambda b, s: (b, 0, 0)),
            pl.BlockSpec((1, WINDOW, D_KV), lambda b, s: (b, 0, 0)),
            pl.BlockSpec((1, CONV_STATE, d_conv), lambda b, s: (b, 0, 0)),
        ],
        out_shape=[
            jax.ShapeDtypeStruct((B, S, D), jnp.float32),
            jax.ShapeDtypeStruct((B, WINDOW, D_KV), jnp.float32),
            jax.ShapeDtypeStruct((B, WINDOW, D_KV), jnp.float32),
            jax.ShapeDtypeStruct((B, CONV_STATE, d_conv), jnp.float32),
        ],
        scratch_shapes=[
            pltpu.VMEM((tile + ATT_BLOCK, D_KV), jnp.bfloat16),
            pltpu.VMEM((tile + ATT_BLOCK, D_KV), jnp.bfloat16),
            pltpu.VMEM((tile + CONV_PAD, d_conv), jnp.float32),
            pltpu.VMEM((tile, d_mix), jnp.bfloat16),
        ],
        compiler_params=pltpu.CompilerParams(
            dimension_semantics=("arbitrary", "arbitrary"), vmem_limit_bytes=VMEM_LIMIT),
        name="mixer_prompt",
    )(sinks, x, nmix, win, qg, kg, dww, dwb, lng, lnb, wout)


def _mixer_sample_kernel(sinks_ref, x_ref, ck_ref, cv_ref, st_ref, nmix_ref, win_ref, qg_ref, kg_ref, dww_ref,
                         dwb_ref, lng_ref, lnb_ref, wout_ref, x1_ref, nk_ref, nv_ref, nc_ref,
                         qbuf, knbuf, vnbuf, glubuf, mixbuf):
    nb = x_ref.shape[0]
    d_conv = st_ref.shape[2]
    x = x_ref[...]
    h = _rms(x, nmix_ref[...])
    z = jnp.dot(h.astype(jnp.bfloat16), win_ref[...], preferred_element_type=jnp.float32)
    o1, o2, o3, o4 = D_ATTN, D_ATTN + D_KV, D_ATTN + 2 * D_KV, D_ATTN + 2 * D_KV + d_conv
    lane_lo = lax.broadcasted_iota(jnp.int32, (nb, LANES), 1) < HEAD_DIM
    qcols = _head_rms(z[:, :o1], qg_ref[...], lane_lo)
    kn = _head_rms(z[:, o1:o2], kg_ref[...], lane_lo)[0]
    v = z[:, o2:o3]
    glu = z[:, o3:o4] * jax.nn.sigmoid(z[:, o4:])
    for c in range(D_ATTN // LANES):
        qbuf[:, c * LANES:(c + 1) * LANES] = qcols[c] * (HEAD_DIM ** -0.5)
    knbuf[...] = kn
    vnbuf[...] = v
    glubuf[...] = glu

    row_lo = lax.broadcasted_iota(jnp.int32, (1, LANES), 1) < HEAD_DIM
    hrow = lax.broadcasted_iota(jnp.int32, (N_Q_HEADS, LANES), 0)
    hlane = lax.broadcasted_iota(jnp.int32, (N_Q_HEADS, LANES), 1)
    keep = (hrow % 2) == (hlane // HEAD_DIM)
    jcol = lax.broadcasted_iota(jnp.int32, (N_Q_HEADS, WINDOW), 1)
    distf = (WINDOW - jcol).astype(jnp.float32)
    hcol = lax.broadcasted_iota(jnp.int32, (N_Q_HEADS, 1), 0)
    slope = jnp.zeros((N_Q_HEADS, 1), jnp.float32)
    sink = jnp.zeros((N_Q_HEADS, 1), jnp.float32)
    for r in range(N_Q_HEADS):
        slope = jnp.where(hcol == r, SLOPES[HEAD_ORDER[r]], slope)
        sink = jnp.where(hcol == r, sinks_ref[HEAD_ORDER[r]], sink)
    bias = slope * distf
    dw_hist = dww_ref[0:CONV_STATE, :]
    dw_last = dww_ref[CONV_STATE:CONV_WIDTH, :]

    def one_seq(b):
        qrow = qbuf[pl.ds(b, 1), :]
        qp = jnp.concatenate(
            [jnp.broadcast_to(qrow[:, c * LANES:(c + 1) * LANES], (2, LANES)) for c in range(D_ATTN // LANES)], axis=0)
        qp = jnp.where(keep, qp, 0.0)
        ckb = ck_ref[b]
        cvb = cv_ref[b]
        knr = knbuf[pl.ds(b, 1), :]
        vnr = vnbuf[pl.ds(b, 1), :]
        glur = glubuf[pl.ds(b, 1), :]
        stb = st_ref[b]

        nk_ref[b, 0:WINDOW - 1, :] = ckb[1:WINDOW, :]
        nk_ref[b, WINDOW - 1:WINDOW, :] = knr
        nv_ref[b, 0:WINDOW - 1, :] = cvb[1:WINDOW, :]
        nv_ref[b, WINDOW - 1:WINDOW, :] = vnr
        nc_ref[b, 0:CONV_STATE - 1, :] = stb[1:CONV_STATE, :]
        nc_ref[b, CONV_STATE - 1:CONV_STATE, :] = glur

        s = lax.dot_general(qp.astype(jnp.bfloat16), ckb.astype(jnp.bfloat16), (((1,), (1,)), ((), ())),
                            preferred_element_type=jnp.float32) - bias
        s_new = jnp.sum(qp * knr, axis=-1, keepdims=True)
        m = jnp.maximum(jnp.maximum(jnp.max(s, axis=-1, keepdims=True), s_new), sink)
        ex = jnp.exp(s - m)
        ex_new = jnp.exp(s_new - m)
        den = jnp.sum(ex, axis=-1, keepdims=True) + ex_new + jnp.exp(sink - m)
        o = jnp.dot((ex / den).astype(jnp.bfloat16), cvb.astype(jnp.bfloat16), preferred_element_type=jnp.float32)
        o = o + (ex_new / den) * vnr
        att = [jnp.where(row_lo, o[2 * c:2 * c + 1, :], o[2 * c + 1:2 * c + 2, :]) for c in range(D_ATTN // LANES)]
        conv = jnp.sum(stb * dw_hist, axis=0, keepdims=True) + glur * dw_last
        return jnp.concatenate(att + [conv], axis=1)

    def per_group(gi, carry):
        base = pl.multiple_of(gi * SUBLANES, SUBLANES)
        mixbuf[pl.ds(base, SUBLANES), :] = jnp.concatenate([one_seq(base + u) for u in range(SUBLANES)], axis=0)
        return carry

    lax.fori_loop(0, nb // SUBLANES, per_group, 0)

    cvo = _layernorm_silu(mixbuf[:, D_ATTN:] + dwb_ref[...], lng_ref[...], lnb_ref[...])
    mixbuf[:, D_ATTN:] = cvo
    x1_ref[...] = x + jnp.dot(mixbuf[...].astype(jnp.bfloat16), wout_ref[...], preferred_element_type=jnp.float32)


def _mixer_sample(x, ck, cv, st, sinks, nmix, win, qg, kg, dww, dwb, lng, lnb, wout, nb):
    DB, D = x.shape
    d_in = win.shape[1]
    d_conv = dww.shape[1]
    d_mix = wout.shape[0]
    full = lambda shape: pl.BlockSpec(shape, lambda i: (0,) * len(shape))
    return pl.pallas_call(
        _mixer_sample_kernel,
        grid=(DB // nb,),
        in_specs=[
            pl.BlockSpec(memory_space=pltpu.SMEM),
            pl.BlockSpec((nb, D), lambda i: (i, 0)),
            pl.BlockSpec((nb, WINDOW, D_KV), lambda i: (i, 0, 0)),
            pl.BlockSpec((nb, WINDOW, D_KV), lambda i: (i, 0, 0)),
            pl.BlockSpec((nb, CONV_STATE, d_conv), lambda i: (i, 0, 0)),
            full((1, D)), full((D, d_in)), full((1, D_ATTN)), full((1, D_KV)),
            full((CONV_WIDTH, d_conv)), full((1, d_conv)), full((1, d_conv)), full((1, d_conv)),
            full((d_mix, D)),
        ],
        out_specs=[
            pl.BlockSpec((nb, D), lambda i: (i, 0)),
            pl.BlockSpec((nb, WINDOW, D_KV), lambda i: (i, 0, 0)),
            pl.BlockSpec((nb, WINDOW, D_KV), lambda i: (i, 0, 0)),
            pl.BlockSpec((nb, CONV_STATE, d_conv), lambda i: (i, 0, 0)),
        ],
        out_shape=[
            jax.ShapeDtypeStruct((DB, D), jnp.float32),
            jax.ShapeDtypeStruct((DB, WINDOW, D_KV), jnp.float32),
            jax.ShapeDtypeStruct((DB, WINDOW, D_KV), jnp.float32),
            jax.ShapeDtypeStruct((DB, CONV_STATE, d_conv), jnp.float32),
        ],
        scratch_shapes=[
            pltpu.VMEM((nb, D_ATTN), jnp.float32),
            pltpu.VMEM((nb, D_KV), jnp.float32),
            pltpu.VMEM((nb, D_KV), jnp.float32),
            pltpu.VMEM((nb, d_conv), jnp.float32),
            pltpu.VMEM((nb, d_mix), jnp.float32),
        ],
        compiler_params=pltpu.CompilerParams(dimension_semantics=("arbitrary",), vmem_limit_bytes=VMEM_LIMIT),
        name="mixer_sample",
    )(sinks, x, ck, cv, st, nmix, win, qg, kg, dww, dwb, lng, lnb, wout)


def _top16_rows(ss, row_id, big):
    ss = list(ss)
    vals = [[] for _ in ss]
    ids = [[] for _ in ss]
    for _ in range(PEER_TOPK):
        for n, s in enumerate(ss):
            m = jnp.max(s, axis=0, keepdims=True)
            im = jnp.min(jnp.where(s == m, row_id, big), axis=0, keepdims=True)
            vals[n].append(m)
            ids[n].append(im)
            ss[n] = jnp.where(row_id == im, -jnp.inf, s)
    return [jnp.concatenate(v, axis=0) for v in vals], [jnp.concatenate(i, axis=0) for i in ids]


def _route_kernel(x1_ref, nffn_ref, wqT_ref, sk_ref, cpos_ref, eid_ref, g_ref, qT):
    T = x1_ref.shape[0]
    nchunk = T // LANES
    par = 2 if nchunk % 2 == 0 else 1
    h2 = _rms(x1_ref[...], nffn_ref[...])
    qT[...] = lax.dot_general(wqT_ref[...], h2.astype(jnp.bfloat16), (((1,), (1,)), ((), ())),
                              preferred_element_type=jnp.float32).astype(qT.dtype)
    key_id = lax.broadcasted_iota(jnp.int32, (N_KEYS, LANES), 0)
    cpos = cpos_ref[...]
    neg_pad = jnp.full((N_CAND_PAD - N_CAND, LANES), -jnp.inf, jnp.float32)
    zero_pad = jnp.zeros((N_CAND_PAD - N_CAND, LANES), jnp.int32)

    def body(i, carry):
        hh = i // (nchunk // par)
        c0 = (i % (nchunk // par)) * par
        rows = pl.ds(pl.multiple_of(hh * 2 * D_HALF, 2 * D_HALF), 2 * D_HALF)
        cols = [pl.ds(pl.multiple_of((c0 + n) * LANES, LANES), LANES) for n in range(par)]
        qb = [qT[rows, cl] for cl in cols]
        sa = [jnp.dot(sk_ref[0], q[:D_HALF], preferred_element_type=jnp.float32) for q in qb]
        sb = [jnp.dot(sk_ref[1], q[D_HALF:], preferred_element_type=jnp.float32) for q in qb]
        va, ia = _top16_rows(sa, key_id, N_KEYS)
        vb, ib = _top16_rows(sb, key_id, N_KEYS)
        cands, cids = [], []
        for n in range(par):
            cand, cid = [], []
            for a in range(PEER_TOPK):
                nbb = PEER_TOPK // (a + 1)
                cand.append(va[n][a:a + 1] + vb[n][0:nbb])
                cid.append(ia[n][a:a + 1] * N_KEYS + ib[n][0:nbb])
            cands.append(jnp.concatenate(cand + [neg_pad], axis=0))
            cids.append(jnp.concatenate(cid + [zero_pad], axis=0))
        tops = [[] for _ in range(par)]
        eids = [[] for _ in range(par)]
        for _ in range(PEER_TOPK):
            for n in range(par):
                m = jnp.max(cands[n], axis=0, keepdims=True)
                pm = jnp.min(jnp.where(cands[n] == m, cpos, PEER_TOPK * PEER_TOPK), axis=0, keepdims=True)
                hit = cpos == pm
                tops[n].append(m)
                eids[n].append(jnp.max(jnp.where(hit, cids[n], -1), axis=0, keepdims=True))
                cands[n] = jnp.where(hit, -jnp.inf, cands[n])
        orow = pl.ds(pl.multiple_of(hh * PEER_TOPK, PEER_TOPK), PEER_TOPK)
        for n in range(par):
            ts = jnp.concatenate(tops[n], axis=0)
            ex = jnp.exp(ts - ts[0:1])
            eid_ref[orow, cols[n]] = jnp.concatenate(eids[n], axis=0)
            g_ref[orow, cols[n]] = ex / jnp.sum(ex, axis=0, keepdims=True)
        return carry

    lax.fori_loop(0, PEER_HEADS * nchunk // par, body, 0)


def _route(x1, nffn, wqT, sk, cpos, tile):
    N, D = x1.shape
    nq = wqT.shape[0]
    full = lambda shape: pl.BlockSpec(shape, lambda i: (0,) * len(shape))
    rows = PEER_HEADS * PEER_TOPK
    return pl.pallas_call(
        _route_kernel,
        grid=(N // tile,),
        in_specs=[
            pl.BlockSpec((tile, D), lambda i: (i, 0)),
            full((1, D)), full((nq, D)), full((2, N_KEYS, D_HALF)), full((N_CAND_PAD, LANES)),
        ],
        out_specs=[
            pl.BlockSpec((rows, tile), lambda i: (0, i)),
            pl.BlockSpec((rows, tile), lambda i: (0, i)),
        ],
        out_shape=[
            jax.ShapeDtypeStruct((rows, N), jnp.int32),
            jax.ShapeDtypeStruct((rows, N), jnp.float32),
        ],
        scratch_shapes=[pltpu.VMEM((nq, tile), jnp.bfloat16)],
        compiler_params=pltpu.CompilerParams(dimension_semantics=("arbitrary",), vmem_limit_bytes=VMEM_LIMIT),
        name="peer_route",
    )(x1, nffn, wqT, sk, cpos)


GROUP = SUBLANES
N_SLOTS = 2 * GROUP
DMA_THREADS = 2
BITREV = [0, 4, 2, 6, 1, 5, 3, 7]


def _sublane_sums(ps, sub):
    def merge(a, b, keep_a, shift):
        x = jnp.where(keep_a, a, pltpu.roll(b, shift, 0))
        y = jnp.where(keep_a, pltpu.roll(a, SUBLANES - shift, 0), b)
        return x + y

    z = [merge(ps[2 * i], ps[2 * i + 1], sub < 4, 4) for i in range(4)]
    w = [merge(z[2 * i], z[2 * i + 1], (sub % 4) < 2, 2) for i in range(2)]
    return merge(w[0], w[1], (sub % 2) < 1, 1)


def _experts_kernel(eid_ref, eid_next_ref, g_ref, x1_ref, nffn_ref, slot_like_ref, tab_ref, out_ref, buf, h2buf,
                    sem):
    step = pl.program_id(0)
    T = x1_ref.shape[0]
    n_groups = T // GROUP
    rows = PEER_HEADS * PEER_TOPK
    d = x1_ref.shape[1] * x1_ref.shape[2]

    x = x1_ref[...]
    ms = jnp.sum(jnp.sum(x * x, axis=2, keepdims=True), axis=1, keepdims=True) * (1.0 / d)
    h2buf[...] = x * lax.rsqrt(ms + EPS) * nffn_ref[...]

    def issue_token(ids_ref, tok, slot):
        for j in range(rows):
            pltpu.make_async_copy(tab_ref.at[ids_ref[tok, j]], buf.at[slot, j // SUBLANES, :, j % SUBLANES],
                                  sem.at[slot]).start(priority=j % DMA_THREADS)

    def wait_token(slot):
        pltpu.make_async_copy(slot_like_ref, buf.at[slot], sem.at[slot]).wait()

    lane_id = lax.broadcasted_iota(jnp.int32, (rows, T), 1)
    sub = lax.broadcasted_iota(jnp.int32, (SUBLANES, LANES), 0)
    n_chunks = x1_ref.shape[1]
    inv_sqrt2 = 1.0 / math.sqrt(2.0)

    def compute_token(t, slot):
        h = h2buf[t]
        gcol = jnp.sum(jnp.where(lane_id == t, g_ref[...], 0.0), axis=1, keepdims=True)
        hq = [jnp.broadcast_to(h[q:q + 1, :], (SUBLANES, LANES)) for q in range(n_chunks)]
        acc = [jnp.zeros((SUBLANES, LANES), jnp.float32) for _ in range(n_chunks)]
        for jg in range(rows // SUBLANES):
            r0 = jg * SUBLANES
            prod = [buf[slot, jg, q] * hq[q] for q in range(n_chunks)]
            while len(prod) > 1:
                prod = [prod[k] + prod[k + 1] for k in range(0, len(prod), 2)]
            sc = jnp.sum(prod[0], axis=1, keepdims=True)
            act = 0.5 * sc * (1.0 + lax.erf(sc * inv_sqrt2))
            wb = jnp.broadcast_to(gcol[r0:r0 + SUBLANES] * act, (SUBLANES, LANES))
            for q in range(n_chunks):
                acc[q] = acc[q] + buf[slot, jg, n_chunks + q] * wb
        out_ref[t] = _sublane_sums([acc[BITREV[k]] for k in range(SUBLANES)], sub)

    def run_group(gi, par, issue_next):
        for u in range(GROUP):
            if issue_next is not None:
                issue_next(u, (1 - par) * GROUP + u)
            slot = par * GROUP + u
            wait_token(slot)
            compute_token(gi * GROUP + u, slot)

    @pl.when(step == 0)
    def _():
        for u in range(GROUP):
            issue_token(eid_ref, u, u)

    def body(gi, carry):
        run_group(gi, gi % 2, lambda u, slot: issue_token(eid_ref, (gi + 1) * GROUP + u, slot))
        return carry

    lax.fori_loop(0, n_groups - 1, body, 0)

    last_par = (n_groups - 1) % 2

    @pl.when(step + 1 < pl.num_programs(0))
    def _():
        for u in range(GROUP):
            issue_token(eid_next_ref, u, (1 - last_par) * GROUP + u)

    run_group(n_groups - 1, last_par, None)


def _experts(eidT, gT, x1, nffn, tab, tile):
    N, D = x1.shape
    rows = PEER_HEADS * PEER_TOPK
    chunks = D // LANES
    assert chunks == SUBLANES and (tile // GROUP) % 2 == 0
    nsteps = N // tile
    eid = eidT.T
    slot_shape = (rows // SUBLANES, 2 * chunks, SUBLANES, LANES)
    slot_like = jnp.zeros(slot_shape, jnp.float32)
    out = pl.pallas_call(
        _experts_kernel,
        grid=(nsteps,),
        in_specs=[
            pl.BlockSpec((tile, rows), lambda i: (i, 0), memory_space=pltpu.SMEM),
            pl.BlockSpec((tile, rows), lambda i: (jnp.minimum(i + 1, nsteps - 1), 0), memory_space=pltpu.SMEM),
            pl.BlockSpec((rows, tile), lambda i: (0, i)),
            pl.BlockSpec((tile, chunks, LANES), lambda i: (i, 0, 0)),
            pl.BlockSpec((chunks, LANES), lambda i: (0, 0)),
            pl.BlockSpec(memory_space=pl.ANY),
            pl.BlockSpec(memory_space=pl.ANY),
        ],
        out_specs=pl.BlockSpec((tile, chunks, LANES), lambda i: (i, 0, 0)),
        out_shape=jax.ShapeDtypeStruct((N, chunks, LANES), jnp.float32),
        scratch_shapes=[
            pltpu.VMEM((N_SLOTS,) + slot_shape, jnp.float32),
            pltpu.VMEM((tile, chunks, LANES), jnp.float32),
            pltpu.SemaphoreType.DMA((N_SLOTS,)),
        ],
        compiler_params=pltpu.CompilerParams(dimension_semantics=("arbitrary",), vmem_limit_bytes=VMEM_LIMIT),
        name="peer_experts",
    )(eid, eid, gT, x1.reshape(N, chunks, LANES), nffn.reshape(chunks, LANES), slot_like, tab)
    return out.reshape(N, D)


def _finish_kernel(x1_ref, peer_ref, p_ref, nple_ref, wgate_ref, wproj_ref, y_ref):
    x2 = x1_ref[...] + peer_ref[...]
    hp = _rms(x2, nple_ref[...])
    gate = jax.nn.sigmoid(jnp.dot(hp.astype(jnp.bfloat16), wgate_ref[...], preferred_element_type=jnp.float32))
    proj = jnp.dot(p_ref[...].astype(jnp.bfloat16), wproj_ref[...], preferred_element_type=jnp.float32)
    y_ref[...] = x2 + proj * gate


def _finish(x1, peer, p, nple, wgate, wproj, tile):
    N, D = x1.shape
    dp = p.shape[1]
    full = lambda shape: pl.BlockSpec(shape, lambda i: (0,) * len(shape))
    return pl.pallas_call(
        _finish_kernel,
        grid=(N // tile,),
        in_specs=[
            pl.BlockSpec((tile, D), lambda i: (i, 0)),
            pl.BlockSpec((tile, D), lambda i: (i, 0)),
            pl.BlockSpec((tile, dp), lambda i: (i, 0)),
            full((1, D)), full((D, D)), full((dp, D)),
        ],
        out_specs=pl.BlockSpec((tile, D), lambda i: (i, 0)),
        out_shape=jax.ShapeDtypeStruct((N, D), jnp.float32),
        compiler_params=pltpu.CompilerParams(dimension_semantics=("arbitrary",), vmem_limit_bytes=VMEM_LIMIT),
        name="finish",
    )(x1, peer, p, nple, wgate, wproj)


def _tile_for(n, pref):
    t = min(pref, n)
    while n % t:
        t //= 2
    return t


def kernel(x_prompt, x_sample, cache_k, cache_v, state_conv, p_prompt, p_sample, norm_mix, w_in, q_gain, k_gain,
           attn_sinks, conv_dw_w, conv_dw_b, conv_ln_g, conv_ln_b, w_out, norm_ffn, peer_wq, peer_sub_keys, peer_u,
           peer_v, norm_ple, w_ple_proj, w_ple_gate):
    depth = norm_mix.shape[0]
    assert depth == 1
    B, S, D = x_prompt.shape
    DB, dec_seq, _ = x_sample.shape
    assert dec_seq == 1
    bf = jnp.bfloat16
    i = 0

    qperm = np.concatenate([np.arange(h * HEAD_DIM, (h + 1) * HEAD_DIM) for h in HEAD_ORDER])
    win = jnp.concatenate([w_in[i][:, qperm], w_in[i][:, D_ATTN:]], axis=1).astype(bf)
    wout = jnp.concatenate([w_out[i][qperm], w_out[i][D_ATTN:]], axis=0).astype(bf)
    qg = jnp.tile(q_gain[i], N_Q_HEADS)[None]
    kg = jnp.tile(k_gain[i], N_KV_HEADS)[None]
    sinks = attn_sinks[i]
    nmix = norm_mix[i][None]
    dww, dwb, lng, lnb = conv_dw_w[i], conv_dw_b[i][None], conv_ln_g[i][None], conv_ln_b[i][None]
    nffn = norm_ffn[i][None]
    wqT = peer_wq[i].T.astype(bf)
    sk = peer_sub_keys[i].astype(bf)
    n_exp = peer_u.shape[1]
    tab = jnp.concatenate([peer_u[i].reshape(n_exp, D // LANES, LANES), peer_v[i].reshape(n_exp, D // LANES, LANES)],
                          axis=1)
    nple = norm_ple[i][None]
    wgate = w_ple_gate[i].astype(bf)
    wproj = w_ple_proj[i].astype(bf)
    cpos = np.full((N_CAND_PAD, LANES), PEER_TOPK * PEER_TOPK, np.int32)
    for r, (a, b) in enumerate(CAND_AB):
        cpos[r, :] = a * PEER_TOPK + b
    cpos = jnp.asarray(cpos)

    def tail(x1, p):
        n = x1.shape[0]
        eidT, gT = _route(x1, nffn, wqT, sk, cpos, _tile_for(n, 512))
        peer = _experts(eidT, gT, x1, nffn, tab, _tile_for(n, 128))
        return _finish(x1, peer, p, nple, wgate, wproj, _tile_for(n, 512))

    x1p, nkp, nvp, ncp = _mixer_prompt(x_prompt, sinks, nmix, win, qg, kg, dww, dwb, lng, lnb, wout, _tile_for(S, 512))
    yp = tail(x1p.reshape(B * S, D), p_prompt[i].reshape(B * S, -1)).reshape(B, S, D)

    ck = cache_k[i].reshape(DB, WINDOW, D_KV)
    cv = cache_v[i].reshape(DB, WINDOW, D_KV)
    x1s, nks, nvs, ncs = _mixer_sample(x_sample.reshape(DB, D), ck, cv, state_conv[i], sinks, nmix, win, qg, kg,
                                       dww, dwb, lng, lnb, wout, _tile_for(DB, 32))
    ys = tail(x1s, p_sample[i].reshape(DB, -1)).reshape(DB, 1, D)

    kv5 = lambda a: a.reshape(1, a.shape[0], WINDOW, N_KV_HEADS, HEAD_DIM)
    return (yp, ys, kv5(nkp), kv5(nvp), ncp[None], kv5(nks), kv5(nvs), ncs[None])
```

```python
import functools
import math

import numpy as np
import jax
import jax.numpy as jnp
from jax import lax
from jax.experimental import pallas as pl
from jax.experimental.pallas import tpu as pltpu

HEAD_DIM = 64
N_Q_HEADS = 8
N_KV_HEADS = 2
Q_PER_KV = N_Q_HEADS // N_KV_HEADS
D_ATTN = N_Q_HEADS * HEAD_DIM
D_KV = N_KV_HEADS * HEAD_DIM
WINDOW = 128
ATT_BLOCK = 128
CONV_WIDTH = 31
CONV_STATE = CONV_WIDTH - 1
N_KEYS = 128
PEER_HEADS = 8
PEER_TOPK = 16
D_HALF = 128
EPS = 1e-6
NEG = -1e30
LANES = 128
SUBLANES = 8
CONV_PAD = 32
VMEM_LIMIT = 56 * 1024 * 1024

HEAD_ORDER = [0, 4, 1, 5, 2, 6, 3, 7]
SLOPES = [2.0 ** (-8.0 * (h + 1) / N_Q_HEADS) for h in range(N_Q_HEADS)]

CAND_AB = [(a, b) for a in range(PEER_TOPK) for b in range(PEER_TOPK // (a + 1))]
N_CAND = len(CAND_AB)
N_CAND_PAD = -(-N_CAND // 8) * 8


def _rms(x, g):
    return x * lax.rsqrt(jnp.mean(x * x, axis=-1, keepdims=True) + EPS) * g


def _head_rms(a, gain, lane_lo):
    cols = []
    for c in range(a.shape[1] // LANES):
        ac = a[:, c * LANES:(c + 1) * LANES]
        sq = ac * ac
        lo = jnp.sum(jnp.where(lane_lo, sq, 0.0), axis=-1, keepdims=True)
        hi = jnp.sum(jnp.where(lane_lo, 0.0, sq), axis=-1, keepdims=True)
        ms = jnp.where(lane_lo, lo, hi) * (1.0 / HEAD_DIM)
        cols.append(ac * lax.rsqrt(ms + EPS) * gain[:, c * LANES:(c + 1) * LANES])
    return cols


def _layernorm_silu(y, g, b):
    mu = jnp.mean(y, axis=-1, keepdims=True)
    yc = y - mu
    yn = yc * lax.rsqrt(jnp.mean(yc * yc, axis=-1, keepdims=True) + EPS) * g + b
    return yn * jax.nn.sigmoid(yn)


def _mixer_prompt_kernel(sinks_ref, x_ref, nmix_ref, win_ref, qg_ref, kg_ref, dww_ref, dwb_ref, lng_ref, lnb_ref,
                         wout_ref, x1_ref, nk_ref, nv_ref, nc_ref, kbuf, vbuf, gbuf, mixbuf):
    si = pl.program_id(1)
    T = x_ref.shape[1]
    d_conv = gbuf.shape[1]
    nblk = T // ATT_BLOCK

    @pl.when(si == 0)
    def _():
        kbuf[0:ATT_BLOCK, :] = jnp.zeros((ATT_BLOCK, D_KV), kbuf.dtype)
        vbuf[0:ATT_BLOCK, :] = jnp.zeros((ATT_BLOCK, D_KV), vbuf.dtype)
        gbuf[0:CONV_PAD, :] = jnp.zeros((CONV_PAD, d_conv), gbuf.dtype)

    x = x_ref[0]
    h = _rms(x, nmix_ref[...])
    z = jnp.dot(h.astype(jnp.bfloat16), win_ref[...], preferred_element_type=jnp.float32)
    o1, o2, o3, o4 = D_ATTN, D_ATTN + D_KV, D_ATTN + 2 * D_KV, D_ATTN + 2 * D_KV + d_conv
    lane_lo = lax.broadcasted_iota(jnp.int32, (T, LANES), 1) < HEAD_DIM

    qcols = _head_rms(z[:, :o1], qg_ref[...], lane_lo)
    kn = _head_rms(z[:, o1:o2], kg_ref[...], lane_lo)[0]
    v = z[:, o2:o3]
    glu = z[:, o3:o4] * jax.nn.sigmoid(z[:, o4:])

    kbuf[ATT_BLOCK:ATT_BLOCK + T, :] = kn.astype(kbuf.dtype)
    vbuf[ATT_BLOCK:ATT_BLOCK + T, :] = v.astype(vbuf.dtype)
    gbuf[CONV_PAD:CONV_PAD + T, :] = glu

    @pl.when(si == pl.num_programs(1) - 1)
    def _():
        nk_ref[0] = kn[T - WINDOW:, :]
        nv_ref[0] = v[T - WINDOW:, :]
        nc_ref[0] = glu[T - CONV_STATE:, :]

    rows = 32
    dwb = dwb_ref[...]
    lng = lng_ref[...]
    lnb = lnb_ref[...]
    first = CONV_PAD - CONV_STATE

    def conv_chunk(base):
        acc = jnp.zeros((rows, d_conv), jnp.float32)
        win = gbuf[base:base + rows + CONV_PAD, :]
        for r in range(SUBLANES):
            span = (rows + CONV_PAD - r) // SUBLANES * SUBLANES
            shifted = win[r:r + span, :]
            for w in range(CONV_WIDTH):
                if (first + w) % SUBLANES == r:
                    k = first + w - r
                    acc = acc + shifted[k:k + rows, :] * dww_ref[w:w + 1, :]
        cv = _layernorm_silu(acc + dwb, lng, lnb)
        mixbuf[base:base + rows, D_ATTN:] = cv.astype(mixbuf.dtype)

    ii = lax.broadcasted_iota(jnp.int32, (ATT_BLOCK, 2 * ATT_BLOCK), 0)
    jj = lax.broadcasted_iota(jnp.int32, (ATT_BLOCK, 2 * ATT_BLOCK), 1)
    dist = ATT_BLOCK + ii - jj
    band = (dist >= 0) & (dist <= WINDOW)
    distf = dist.astype(jnp.float32)
    lane_lo_b = lax.broadcasted_iota(jnp.int32, (ATT_BLOCK, LANES), 1) < HEAD_DIM
    scale = HEAD_DIM ** -0.5
    for blk in range(nblk):
        kk = kbuf[blk * ATT_BLOCK:(blk + 2) * ATT_BLOCK, :]
        vv = vbuf[blk * ATT_BLOCK:(blk + 2) * ATT_BLOCK, :]
        if blk == 0:
            valid = band & ((jj >= ATT_BLOCK) | (si > 0))
        else:
            valid = band
        for c in range(D_ATTN // LANES):
            qc = qcols[c][blk * ATT_BLOCK:(blk + 1) * ATT_BLOCK, :] * scale
            outs = []
            for e in range(2):
                head = HEAD_ORDER[2 * c + e]
                qe = jnp.where(lane_lo_b if e == 0 else ~lane_lo_b, qc, 0.0).astype(jnp.bfloat16)
                s = lax.dot_general(qe, kk, (((1,), (1,)), ((), ())), preferred_element_type=jnp.float32)
                s = s - SLOPES[head] * distf
                s = jnp.where(valid, s, NEG)
                sink = sinks_ref[head]
                m = jnp.maximum(jnp.max(s, axis=-1, keepdims=True), sink)
                ex = jnp.exp(s - m)
                den = jnp.sum(ex, axis=-1, keepdims=True) + jnp.exp(sink - m)
                p = ex / den
                outs.append(jnp.dot(p.astype(jnp.bfloat16), vv, preferred_element_type=jnp.float32))
            att = jnp.where(lane_lo_b, outs[0], outs[1])
            mixbuf[blk * ATT_BLOCK:(blk + 1) * ATT_BLOCK, c * LANES:(c + 1) * LANES] = att.astype(mixbuf.dtype)
        for cc in range(ATT_BLOCK // rows):
            conv_chunk(blk * ATT_BLOCK + cc * rows)

    x1_ref[0] = x + jnp.dot(mixbuf[...], wout_ref[...], preferred_element_type=jnp.float32)

    kbuf[0:ATT_BLOCK, :] = kbuf[T:T + ATT_BLOCK, :]
    vbuf[0:ATT_BLOCK, :] = vbuf[T:T + ATT_BLOCK, :]
    gbuf[0:CONV_PAD, :] = gbuf[T:T + CONV_PAD, :]


def _mixer_prompt(x, sinks, nmix, win, qg, kg, dww, dwb, lng, lnb, wout, tile):
    B, S, D = x.shape
    d_in = win.shape[1]
    d_conv = dww.shape[1]
    d_mix = wout.shape[0]
    full = lambda shape: pl.BlockSpec(shape, lambda b, s: (0,) * len(shape))
    return pl.pallas_call(
        _mixer_prompt_kernel,
        grid=(B, S // tile),
        in_specs=[
            pl.BlockSpec(memory_space=pltpu.SMEM),
            pl.BlockSpec((1, tile, D), lambda b, s: (b, s, 0)),
            full((1, D)), full((D, d_in)), full((1, D_ATTN)), full((1, D_KV)),
            full((CONV_WIDTH, d_conv)), full((1, d_conv)), full((1, d_conv)), full((1, d_conv)),
            full((d_mix, D)),
        ],
        out_specs=[
            pl.BlockSpec((1, tile, D), lambda b, s: (b, s, 0)),
            pl.BlockSpec((1, WINDOW, D_KV), lambda b, s: (b, 0, 0)),
            pl.BlockSpec((1, WINDOW, D_KV), lambda b, s: (b, 0, 0)),
            pl.BlockSpec((1, CONV_STATE, d_conv), lambda b, s: (b, 0, 0)),
        ],
        out_shape=[
            jax.ShapeDtypeStruct((B, S, D), jnp.float32),
            jax.ShapeDtypeStruct((B, WINDOW, D_KV), jnp.float32),
            jax.ShapeDtypeStruct((B, WINDOW, D_KV), jnp.float32),
            jax.ShapeDtypeStruct((B, CONV_STATE, d_conv), jnp.float32),
        ],
        scratch_shapes=[
            pltpu.VMEM((tile + ATT_BLOCK, D_KV), jnp.bfloat16),
            pltpu.VMEM((tile + ATT_BLOCK, D_KV), jnp.bfloat16),
            pltpu.VMEM((tile + CONV_PAD, d_conv), jnp.float32),
            pltpu.VMEM((tile, d_mix), jnp.bfloat16),
        ],
        compiler_params=pltpu.CompilerParams(
            dimension_semantics=("arbitrary", "arbitrary"), vmem_limit_bytes=VMEM_LIMIT),
        name="mixer_prompt",
    )(sinks, x, nmix, win, qg, kg, dww, dwb, lng, lnb, wout)


def _mixer_sample_kernel(sinks_ref, x_ref, ck_ref, cv_ref, st_ref, nmix_ref, win_ref, qg_ref, kg_ref, dww_ref,
                         dwb_ref, lng_ref, lnb_ref, wout_ref, x1_ref, nk_ref, nv_ref, nc_ref,
                         qbuf, knbuf, vnbuf, glubuf, mixbuf):
    nb = x_ref.shape[0]
    d_conv = st_ref.shape[2]
    x = x_ref[...]
    h = _rms(x, nmix_ref[...])
    z = jnp.dot(h.astype(jnp.bfloat16), win_ref[...], preferred_element_type=jnp.float32)
    o1, o2, o3, o4 = D_ATTN, D_ATTN + D_KV, D_ATTN + 2 * D_KV, D_ATTN + 2 * D_KV + d_conv
    lane_lo = lax.broadcasted_iota(jnp.int32, (nb, LANES), 1) < HEAD_DIM
    qcols = _head_rms(z[:, :o1], qg_ref[...], lane_lo)
    kn = _head_rms(z[:, o1:o2], kg_ref[...], lane_lo)[0]
    v = z[:, o2:o3]
    glu = z[:, o3:o4] * jax.nn.sigmoid(z[:, o4:])
    for c in range(D_ATTN // LANES):
        qbuf[:, c * LANES:(c + 1) * LANES] = qcols[c] * (HEAD_DIM ** -0.5)
    knbuf[...] = kn
    vnbuf[...] = v
    glubuf[...] = glu

    row_lo = lax.broadcasted_iota(jnp.int32, (1, LANES), 1) < HEAD_DIM
    hrow = lax.broadcasted_iota(jnp.int32, (N_Q_HEADS, LANES), 0)
    hlane = lax.broadcasted_iota(jnp.int32, (N_Q_HEADS, LANES), 1)
    keep = (hrow % 2) == (hlane // HEAD_DIM)
    jcol = lax.broadcasted_iota(jnp.int32, (N_Q_HEADS, WINDOW), 1)
    distf = (WINDOW - jcol).astype(jnp.float32)
    hcol = lax.broadcasted_iota(jnp.int32, (N_Q_HEADS, 1), 0)
    slope = jnp.zeros((N_Q_HEADS, 1), jnp.float32)
    sink = jnp.zeros((N_Q_HEADS, 1), jnp.float32)
    for r in range(N_Q_HEADS):
        slope = jnp.where(hcol == r, SLOPES[HEAD_ORDER[r]], slope)
        sink = jnp.where(hcol == r, sinks_ref[HEAD_ORDER[r]], sink)
    bias = slope * distf
    dw_hist = dww_ref[0:CONV_STATE, :]
    dw_last = dww_ref[CONV_STATE:CONV_WIDTH, :]

    def one_seq(b):
        qrow = qbuf[pl.ds(b, 1), :]
        qp = jnp.concatenate(
            [jnp.broadcast_to(qrow[:, c * LANES:(c + 1) * LANES], (2, LANES)) for c in range(D_ATTN // LANES)], axis=0)
        qp = jnp.where(keep, qp, 0.0)
        ckb = ck_ref[b]
        cvb = cv_ref[b]
        knr = knbuf[pl.ds(b, 1), :]
        vnr = vnbuf[pl.ds(b, 1), :]
        glur = glubuf[pl.ds(b, 1), :]
        stb = st_ref[b]

        nk_ref[b, 0:WINDOW - 1, :] = ckb[1:WINDOW, :]
        nk_ref[b, WINDOW - 1:WINDOW, :] = knr
        nv_ref[b, 0:WINDOW - 1, :] = cvb[1:WINDOW, :]
        nv_ref[b, WINDOW - 1:WINDOW, :] = vnr
        nc_ref[b, 0:CONV_STATE - 1, :] = stb[1:CONV_STATE, :]
        nc_ref[b, CONV_STATE - 1:CONV_STATE, :] = glur

        s = lax.dot_general(qp.astype(jnp.bfloat16), ckb.astype(jnp.bfloat16), (((1,), (1,)), ((), ())),
                            preferred_element_type=jnp.float32) - bias
        s_new = jnp.sum(qp * knr, axis=-1, keepdims=True)
        m = jnp.maximum(jnp.maximum(jnp.max(s, axis=-1, keepdims=True), s_new), sink)
        ex = jnp.exp(s - m)
        ex_new = jnp.exp(s_new - m)
        den = jnp.sum(ex, axis=-1, keepdims=True) + ex_new + jnp.exp(sink - m)
        o = jnp.dot((ex / den).astype(jnp.bfloat16), cvb.astype(jnp.bfloat16), preferred_element_type=jnp.float32)
        o = o + (ex_new / den) * vnr
        att = [jnp.where(row_lo, o[2 * c:2 * c + 1, :], o[2 * c + 1:2 * c + 2, :]) for c in range(D_ATTN // LANES)]
        conv = jnp.sum(stb * dw_hist, axis=0, keepdims=True) + glur * dw_last
        return jnp.concatenate(att + [conv], axis=1)

    def per_group(gi, carry):
        base = pl.multiple_of(gi * SUBLANES, SUBLANES)
        mixbuf[pl.ds(base, SUBLANES), :] = jnp.concatenate([one_seq(base + u) for u in range(SUBLANES)], axis=0)
        return carry

    lax.fori_loop(0, nb // SUBLANES, per_group, 0)

    cvo = _layernorm_silu(mixbuf[:, D_ATTN:] + dwb_ref[...], lng_ref[...], lnb_ref[...])
    mixbuf[:, D_ATTN:] = cvo
    x1_ref[...] = x + jnp.dot(mixbuf[...].astype(jnp.bfloat16), wout_ref[...], preferred_element_type=jnp.float32)


def _mixer_sample(x, ck, cv, st, sinks, nmix, win, qg, kg, dww, dwb, lng, lnb, wout, nb):
    DB, D = x.shape
    d_in = win.shape[1]
    d_conv = dww.shape[1]
    d_mix = wout.shape[0]
    full = lambda shape: pl.BlockSpec(shape, lambda i: (0,) * len(shape))
    return pl.pallas_call(
        _mixer_sample_kernel,
        grid=(DB // nb,),
        in_specs=[
            pl.BlockSpec(memory_space=pltpu.SMEM),
            pl.BlockSpec((nb, D), lambda i: (i, 0)),
            pl.BlockSpec((nb, WINDOW, D_KV), lambda i: (i, 0, 0)),
            pl.BlockSpec((nb, WINDOW, D_KV), lambda i: (i, 0, 0)),
            pl.BlockSpec((nb, CONV_STATE, d_conv), lambda i: (i, 0, 0)),
            full((1, D)), full((D, d_in)), full((1, D_ATTN)), full((1, D_KV)),
            full((CONV_WIDTH, d_conv)), full((1, d_conv)), full((1, d_conv)), full((1, d_conv)),
            full((d_mix, D)),
        ],
        out_specs=[
            pl.BlockSpec((nb, D), lambda i: (i, 0)),
            pl.BlockSpec((nb, WINDOW, D_KV), lambda i: (i, 0, 0)),
            pl.BlockSpec((nb, WINDOW, D_KV), lambda i: (i, 0, 0)),
            pl.BlockSpec((nb, CONV_STATE, d_conv), lambda i: (i, 0, 0)),
        ],
        out_shape=[
            jax.ShapeDtypeStruct((DB, D), jnp.float32),
            jax.ShapeDtypeStruct((DB, WINDOW, D_KV), jnp.float32),
            jax.ShapeDtypeStruct((DB, WINDOW, D_KV), jnp.float32),
            jax.ShapeDtypeStruct((DB, CONV_STATE, d_conv), jnp.float32),
        ],
        scratch_shapes=[
            pltpu.VMEM((nb, D_ATTN), jnp.float32),
            pltpu.VMEM((nb, D_KV), jnp.float32),
            pltpu.VMEM((nb, D_KV), jnp.float32),
            pltpu.VMEM((nb, d_conv), jnp.float32),
            pltpu.VMEM((nb, d_mix), jnp.float32),
        ],
        compiler_params=pltpu.CompilerParams(dimension_semantics=("arbitrary",), vmem_limit_bytes=VMEM_LIMIT),
        name="mixer_sample",
    )(sinks, x, ck, cv, st, nmix, win, qg, kg, dww, dwb, lng, lnb, wout)


GROUP = SUBLANES
N_SLOTS = 2 * GROUP
DMA_THREADS = 2
BITREV = [0, 4, 2, 6, 1, 5, 3, 7]
PEER_TILE = 2 * GROUP * PEER_HEADS


def _top16_rows(s, row_id, big):
    vals, ids = [], []
    for _ in range(PEER_TOPK):
        m = jnp.max(s, axis=0, keepdims=True)
        im = jnp.min(jnp.where(s == m, row_id, big), axis=0, keepdims=True)
        vals.append(m)
        ids.append(im)
        s = jnp.where(row_id == im, -jnp.inf, s)
    return jnp.concatenate(vals, axis=0), jnp.concatenate(ids, axis=0)


def _sublane_sums(ps, sub):
    def merge(a, b, keep_a, shift):
        x = jnp.where(keep_a, a, pltpu.roll(b, shift, 0))
        y = jnp.where(keep_a, pltpu.roll(a, SUBLANES - shift, 0), b)
        return x + y

    z = [merge(ps[2 * i], ps[2 * i + 1], sub < 4, 4) for i in range(4)]
    w = [merge(z[2 * i], z[2 * i + 1], (sub % 4) < 2, 2) for i in range(2)]
    return merge(w[0], w[1], (sub % 2) < 1, 1)


def _peer_kernel(x1c_ref, x1next_ref, x1first_ref, nffnc_ref, nffn_ref, wqT_ref, sk_ref, cpos_ref, slot_like_ref,
                 tab_ref, out_ref, buf, h2buf, qT, eid_hk, ids_v, ids_s, gates, sem, ids_sem):
    step = pl.program_id(0)
    nsteps = pl.num_programs(0)
    T = x1c_ref.shape[0]
    rows = PEER_HEADS * PEER_TOPK
    n_chunks = x1c_ref.shape[1]
    d = n_chunks * x1c_ref.shape[2]
    cur = step % 2

    key_id = lax.broadcasted_iota(jnp.int32, (N_KEYS, LANES), 0)
    neg_pad = jnp.full((N_CAND_PAD - N_CAND, LANES), -jnp.inf, jnp.float32)
    zero_pad = jnp.zeros((N_CAND_PAD - N_CAND, LANES), jnp.int32)

    def route_project(x_ref):
        h2 = _rms(x_ref[...], nffn_ref[...])
        qT[...] = lax.dot_general(wqT_ref[...], h2.astype(jnp.bfloat16), (((1,), (1,)), ((), ())),
                                  preferred_element_type=jnp.float32).astype(qT.dtype)

    def route_head(hh, dst):
        cpos = cpos_ref[...]
        qb = qT[pl.ds(pl.multiple_of(hh * 2 * D_HALF, 2 * D_HALF), 2 * D_HALF), :]
        sa = jnp.dot(sk_ref[0], qb[:D_HALF], preferred_element_type=jnp.float32)
        sb = jnp.dot(sk_ref[1], qb[D_HALF:], preferred_element_type=jnp.float32)
        va, ia = _top16_rows(sa, key_id, N_KEYS)
        vb, ib = _top16_rows(sb, key_id, N_KEYS)
        cand, cid = [], []
        for a in range(PEER_TOPK):
            nbb = PEER_TOPK // (a + 1)
            cand.append(va[a:a + 1] + vb[0:nbb])
            cid.append(ia[a:a + 1] * N_KEYS + ib[0:nbb])
        cand = jnp.concatenate(cand + [neg_pad], axis=0)
        cid = jnp.concatenate(cid + [zero_pad], axis=0)
        tops, eids = [], []
        for _ in range(PEER_TOPK):
            m = jnp.max(cand, axis=0, keepdims=True)
            pm = jnp.min(jnp.where(cand == m, cpos, PEER_TOPK * PEER_TOPK), axis=0, keepdims=True)
            hit = cpos == pm
            tops.append(m)
            eids.append(jnp.max(jnp.where(hit, cid, -1), axis=0, keepdims=True))
            cand = jnp.where(hit, -jnp.inf, cand)
        ts = jnp.concatenate(tops, axis=0)
        ex = jnp.exp(ts - ts[0:1])
        orow = pl.ds(pl.multiple_of(hh * PEER_TOPK, PEER_TOPK), PEER_TOPK)
        eid_hk[orow, :] = jnp.concatenate(eids, axis=0)
        gates[dst, orow, :] = ex / jnp.sum(ex, axis=0, keepdims=True)

    def ids_copy(dst):
        return pltpu.make_async_copy(ids_v, ids_s.at[dst], ids_sem)

    def route_publish(dst):
        ids_v[...] = eid_hk[...].T
        ids_copy(dst).start()

    def issue_token(src, tok, slot):
        for j in range(rows):
            pltpu.make_async_copy(tab_ref.at[ids_s[src, tok, j]], buf.at[slot, j // SUBLANES, :, j % SUBLANES],
                                  sem.at[slot]).start(priority=j % DMA_THREADS)

    def wait_token(slot):
        pltpu.make_async_copy(slot_like_ref, buf.at[slot], sem.at[slot]).wait()

    lane_id = lax.broadcasted_iota(jnp.int32, (rows, T), 1)
    sub = lax.broadcasted_iota(jnp.int32, (SUBLANES, LANES), 0)
    inv_sqrt2 = 1.0 / math.sqrt(2.0)

    def compute_token(t, slot):
        h = h2buf[t]
        gcol = jnp.sum(jnp.where(lane_id == t, gates[cur], 0.0), axis=1, keepdims=True)
        hq = [jnp.broadcast_to(h[q:q + 1, :], (SUBLANES, LANES)) for q in range(n_chunks)]
        acc = [jnp.zeros((SUBLANES, LANES), jnp.float32) for _ in range(n_chunks)]
        for jg in range(rows // SUBLANES):
            r0 = jg * SUBLANES
            prod = [buf[slot, jg, q] * hq[q] for q in range(n_chunks)]
            while len(prod) > 1:
                prod = [prod[k] + prod[k + 1] for k in range(0, len(prod), 2)]
            sc = jnp.sum(prod[0], axis=1, keepdims=True)
            act = 0.5 * sc * (1.0 + lax.erf(sc * inv_sqrt2))
            wb = jnp.broadcast_to(gcol[r0:r0 + SUBLANES] * act, (SUBLANES, LANES))
            for q in range(n_chunks):
                acc[q] = acc[q] + buf[slot, jg, n_chunks + q] * wb
        out_ref[t] = _sublane_sums([acc[BITREV[k]] for k in range(SUBLANES)], sub)

    def run_group(first_tok, par, issue_next):
        for u in range(GROUP):
            if issue_next is not None:
                issue_next(u, (1 - par) * GROUP + u)
            slot = par * GROUP + u
            wait_token(slot)
            compute_token(first_tok + u, slot)

    @pl.when(step == 0)
    def _():
        route_project(x1first_ref)

        def head(hh, carry):
            route_head(hh, 0)
            return carry

        lax.fori_loop(0, PEER_HEADS, head, 0)
        route_publish(0)
        ids_copy(0).wait()
        for u in range(GROUP):
            issue_token(0, u, u)

    x = x1c_ref[...]
    ms = jnp.sum(jnp.sum(x * x, axis=2, keepdims=True), axis=1, keepdims=True) * (1.0 / d)
    h2buf[...] = x * lax.rsqrt(ms + EPS) * nffnc_ref[...]
    route_project(x1next_ref)

    def first_group(first_tok):
        run_group(first_tok, 0, lambda u, slot: issue_token(cur, first_tok + GROUP + u, slot))

    def body(pi, carry):
        first_tok = pi * 2 * GROUP
        first_group(first_tok)
        run_group(first_tok + GROUP, 1, lambda u, slot: issue_token(cur, first_tok + 2 * GROUP + u, slot))
        route_head(pi, 1 - cur)
        return carry

    lax.fori_loop(0, PEER_HEADS - 1, body, 0)

    last_tok = (PEER_HEADS - 1) * 2 * GROUP
    route_head(PEER_HEADS - 1, 1 - cur)
    route_publish(1 - cur)
    first_group(last_tok)
    ids_copy(1 - cur).wait()

    @pl.when(step + 1 < nsteps)
    def _():
        for u in range(GROUP):
            issue_token(1 - cur, u, u)

    run_group(last_tok + GROUP, 1, None)


def _peer(x1, nffn, wqT, sk, cpos, tab):
    N, D = x1.shape
    T = PEER_TILE
    rows = PEER_HEADS * PEER_TOPK
    chunks = D // LANES
    nq = wqT.shape[0]
    assert chunks == SUBLANES and N % T == 0 and T == LANES
    nsteps = N // T
    slot_shape = (rows // SUBLANES, 2 * chunks, SUBLANES, LANES)
    slot_like = jnp.zeros(slot_shape, jnp.float32)
    full = lambda shape: pl.BlockSpec(shape, lambda i: (0,) * len(shape))
    out = pl.pallas_call(
        _peer_kernel,
        grid=(nsteps,),
        in_specs=[
            pl.BlockSpec((T, chunks, LANES), lambda i: (i, 0, 0)),
            pl.BlockSpec((T, D), lambda i: (jnp.minimum(i + 1, nsteps - 1), 0)),
            pl.BlockSpec((T, D), lambda i: (0, 0)),
            full((chunks, LANES)), full((1, D)), full((nq, D)), full((2, N_KEYS, D_HALF)), full((N_CAND_PAD, LANES)),
            pl.BlockSpec(memory_space=pl.ANY),
            pl.BlockSpec(memory_space=pl.ANY),
        ],
        out_specs=pl.BlockSpec((T, chunks, LANES), lambda i: (i, 0, 0)),
        out_shape=jax.ShapeDtypeStruct((N, chunks, LANES), jnp.float32),
        scratch_shapes=[
            pltpu.VMEM((N_SLOTS,) + slot_shape, jnp.float32),
            pltpu.VMEM((T, chunks, LANES), jnp.float32),
            pltpu.VMEM((nq, T), jnp.bfloat16),
            pltpu.VMEM((rows, T), jnp.int32),
            pltpu.VMEM((T, rows), jnp.int32),
            pltpu.SMEM((2, T, rows), jnp.int32),
            pltpu.VMEM((2, rows, T), jnp.float32),
            pltpu.SemaphoreType.DMA((N_SLOTS,)),
            pltpu.SemaphoreType.DMA,
        ],
        compiler_params=pltpu.CompilerParams(dimension_semantics=("arbitrary",), vmem_limit_bytes=VMEM_LIMIT),
        name="peer",
    )(x1.reshape(N, chunks, LANES), x1, x1, nffn.reshape(chunks, LANES), nffn, wqT, sk, cpos, slot_like, tab)
    return out.reshape(N, D)


def _finish_kernel(x1_ref, peer_ref, p_ref, nple_ref, wgate_ref, wproj_ref, y_ref):
    x2 = x1_ref[...] + peer_ref[...]
    hp = _rms(x2, nple_ref[...])
    gate = jax.nn.sigmoid(jnp.dot(hp.astype(jnp.bfloat16), wgate_ref[...], preferred_element_type=jnp.float32))
    proj = jnp.dot(p_ref[...].astype(jnp.bfloat16), wproj_ref[...], preferred_element_type=jnp.float32)
    y_ref[...] = x2 + proj * gate


def _finish(x1, peer, p, nple, wgate, wproj, tile):
    N, D = x1.shape
    dp = p.shape[1]
    full = lambda shape: pl.BlockSpec(shape, lambda i: (0,) * len(shape))
    return pl.pallas_call(
        _finish_kernel,
        grid=(N // tile,),
        in_specs=[
            pl.BlockSpec((tile, D), lambda i: (i, 0)),
            pl.BlockSpec((tile, D), lambda i: (i, 0)),
            pl.BlockSpec((tile, dp), lambda i: (i, 0)),
            full((1, D)), full((D, D)), full((dp, D)),
        ],
        out_specs=pl.BlockSpec((tile, D), lambda i: (i, 0)),
        out_shape=jax.ShapeDtypeStruct((N, D), jnp.float32),
        compiler_params=pltpu.CompilerParams(dimension_semantics=("arbitrary",), vmem_limit_bytes=VMEM_LIMIT),
        name="finish",
    )(x1, peer, p, nple, wgate, wproj)


def _tile_for(n, pref):
    t = min(pref, n)
    while n % t:
        t //= 2
    return t


def kernel(x_prompt, x_sample, cache_k, cache_v, state_conv, p_prompt, p_sample, norm_mix, w_in, q_gain, k_gain,
           attn_sinks, conv_dw_w, conv_dw_b, conv_ln_g, conv_ln_b, w_out, norm_ffn, peer_wq, peer_sub_keys, peer_u,
           peer_v, norm_ple, w_ple_proj, w_ple_gate):
    depth = norm_mix.shape[0]
    assert depth == 1
    B, S, D = x_prompt.shape
    DB, dec_seq, _ = x_sample.shape
    assert dec_seq == 1
    bf = jnp.bfloat16
    i = 0

    qperm = np.concatenate([np.arange(h * HEAD_DIM, (h + 1) * HEAD_DIM) for h in HEAD_ORDER])
    win = jnp.concatenate([w_in[i][:, qperm], w_in[i][:, D_ATTN:]], axis=1).astype(bf)
    wout = jnp.concatenate([w_out[i][qperm], w_out[i][D_ATTN:]], axis=0).astype(bf)
    qg = jnp.tile(q_gain[i], N_Q_HEADS)[None]
    kg = jnp.tile(k_gain[i], N_KV_HEADS)[None]
    sinks = attn_sinks[i]
    nmix = norm_mix[i][None]
    dww, dwb, lng, lnb = conv_dw_w[i], conv_dw_b[i][None], conv_ln_g[i][None], conv_ln_b[i][None]
    nffn = norm_ffn[i][None]
    wqT = peer_wq[i].T.astype(bf)
    sk = peer_sub_keys[i].astype(bf)
    n_exp = peer_u.shape[1]
    tab = jnp.concatenate([peer_u[i].reshape(n_exp, D // LANES, LANES), peer_v[i].reshape(n_exp, D // LANES, LANES)],
                          axis=1)
    nple = norm_ple[i][None]
    wgate = w_ple_gate[i].astype(bf)
    wproj = w_ple_proj[i].astype(bf)
    cpos = np.full((N_CAND_PAD, LANES), PEER_TOPK * PEER_TOPK, np.int32)
    for r, (a, b) in enumerate(CAND_AB):
        cpos[r, :] = a * PEER_TOPK + b
    cpos = jnp.asarray(cpos)

    def tail(x1, p):
        n = x1.shape[0]
        peer = _peer(x1, nffn, wqT, sk, cpos, tab)
        return _finish(x1, peer, p, nple, wgate, wproj, _tile_for(n, 512))

    x1p, nkp, nvp, ncp = _mixer_prompt(x_prompt, sinks, nmix, win, qg, kg, dww, dwb, lng, lnb, wout, _tile_for(S, 512))
    yp = tail(x1p.reshape(B * S, D), p_prompt[i].reshape(B * S, -1)).reshape(B, S, D)

    ck = cache_k[i].reshape(DB, WINDOW, D_KV)
    cv = cache_v[i].reshape(DB, WINDOW, D_KV)
    x1s, nks, nvs, ncs = _mixer_sample(x_sample.reshape(DB, D), ck, cv, state_conv[i], sinks, nmix, win, qg, kg,
                                       dww, dwb, lng, lnb, wout, _tile_for(DB, 32))
    ys = tail(x1s, p_sample[i].reshape(DB, -1)).reshape(DB, 1, D)

    kv5 = lambda a: a.reshape(1, a.shape[0], WINDOW, N_KV_HEADS, HEAD_DIM)
    return (yp, ys, kv5(nkp), kv5(nvp), ncp[None], kv5(nks), kv5(nvs), ncs[None])
```

```python
import functools
import math

import numpy as np
import jax
import jax.numpy as jnp
from jax import lax
from jax.experimental import pallas as pl
from jax.experimental.pallas import tpu as pltpu

HEAD_DIM = 64
N_Q_HEADS = 8
N_KV_HEADS = 2
Q_PER_KV = N_Q_HEADS // N_KV_HEADS
D_ATTN = N_Q_HEADS * HEAD_DIM
D_KV = N_KV_HEADS * HEAD_DIM
WINDOW = 128
ATT_BLOCK = 128
CONV_WIDTH = 31
CONV_STATE = CONV_WIDTH - 1
N_KEYS = 128
PEER_HEADS = 8
PEER_TOPK = 16
D_HALF = 128
EPS = 1e-6
NEG = -1e30
LANES = 128
SUBLANES = 8
CONV_PAD = 32
VMEM_LIMIT = 56 * 1024 * 1024

HEAD_ORDER = [0, 4, 1, 5, 2, 6, 3, 7]
SLOPES = [2.0 ** (-8.0 * (h + 1) / N_Q_HEADS) for h in range(N_Q_HEADS)]

CAND_AB = [(a, b) for a in range(PEER_TOPK) for b in range(PEER_TOPK // (a + 1))]
N_CAND = len(CAND_AB)
N_CAND_PAD = -(-N_CAND // 8) * 8


def _rms(x, g):
    return x * lax.rsqrt(jnp.mean(x * x, axis=-1, keepdims=True) + EPS) * g


def _head_rms(a, gain, lane_lo):
    cols = []
    for c in range(a.shape[1] // LANES):
        ac = a[:, c * LANES:(c + 1) * LANES]
        sq = ac * ac
        lo = jnp.sum(jnp.where(lane_lo, sq, 0.0), axis=-1, keepdims=True)
        hi = jnp.sum(jnp.where(lane_lo, 0.0, sq), axis=-1, keepdims=True)
        ms = jnp.where(lane_lo, lo, hi) * (1.0 / HEAD_DIM)
        cols.append(ac * lax.rsqrt(ms + EPS) * gain[:, c * LANES:(c + 1) * LANES])
    return cols


def _layernorm_silu(y, g, b):
    mu = jnp.mean(y, axis=-1, keepdims=True)
    yc = y - mu
    yn = yc * lax.rsqrt(jnp.mean(yc * yc, axis=-1, keepdims=True) + EPS) * g + b
    return yn * jax.nn.sigmoid(yn)


def _mixer_prompt_kernel(sinks_ref, x_ref, nmix_ref, win_ref, qg_ref, kg_ref, dww_ref, dwb_ref, lng_ref, lnb_ref,
                         wout_ref, x1_ref, nk_ref, nv_ref, nc_ref, kbuf, vbuf, gbuf, mixbuf):
    si = pl.program_id(1)
    T = x_ref.shape[1]
    d_conv = gbuf.shape[1]
    nblk = T // ATT_BLOCK

    @pl.when(si == 0)
    def _():
        kbuf[0:ATT_BLOCK, :] = jnp.zeros((ATT_BLOCK, D_KV), kbuf.dtype)
        vbuf[0:ATT_BLOCK, :] = jnp.zeros((ATT_BLOCK, D_KV), vbuf.dtype)
        gbuf[0:CONV_PAD, :] = jnp.zeros((CONV_PAD, d_conv), gbuf.dtype)

    x = x_ref[0]
    h = _rms(x, nmix_ref[...])
    z = jnp.dot(h.astype(jnp.bfloat16), win_ref[...], preferred_element_type=jnp.float32)
    o1, o2, o3, o4 = D_ATTN, D_ATTN + D_KV, D_ATTN + 2 * D_KV, D_ATTN + 2 * D_KV + d_conv
    lane_lo = lax.broadcasted_iota(jnp.int32, (T, LANES), 1) < HEAD_DIM

    qcols = _head_rms(z[:, :o1], qg_ref[...], lane_lo)
    kn = _head_rms(z[:, o1:o2], kg_ref[...], lane_lo)[0]
    v = z[:, o2:o3]
    glu = z[:, o3:o4] * jax.nn.sigmoid(z[:, o4:])

    kbuf[ATT_BLOCK:ATT_BLOCK + T, :] = kn.astype(kbuf.dtype)
    vbuf[ATT_BLOCK:ATT_BLOCK + T, :] = v.astype(vbuf.dtype)
    gbuf[CONV_PAD:CONV_PAD + T, :] = glu

    @pl.when(si == pl.num_programs(1) - 1)
    def _():
        nk_ref[0] = kn[T - WINDOW:, :]
        nv_ref[0] = v[T - WINDOW:, :]
        nc_ref[0] = glu[T - CONV_STATE:, :]

    rows = 32
    dwb = dwb_ref[...]
    lng = lng_ref[...]
    lnb = lnb_ref[...]
    first = CONV_PAD - CONV_STATE

    def conv_chunk(base):
        acc = jnp.zeros((rows, d_conv), jnp.float32)
        win = gbuf[base:base + rows + CONV_PAD, :]
        for r in range(SUBLANES):
            span = (rows + CONV_PAD - r) // SUBLANES * SUBLANES
            shifted = win[r:r + span, :]
            for w in range(CONV_WIDTH):
                if (first + w) % SUBLANES == r:
                    k = first + w - r
                    acc = acc + shifted[k:k + rows, :] * dww_ref[w:w + 1, :]
        cv = _layernorm_silu(acc + dwb, lng, lnb)
        mixbuf[base:base + rows, D_ATTN:] = cv.astype(mixbuf.dtype)

    ii = lax.broadcasted_iota(jnp.int32, (ATT_BLOCK, 2 * ATT_BLOCK), 0)
    jj = lax.broadcasted_iota(jnp.int32, (ATT_BLOCK, 2 * ATT_BLOCK), 1)
    dist = ATT_BLOCK + ii - jj
    band = (dist >= 0) & (dist <= WINDOW)
    distf = dist.astype(jnp.float32)
    lane_lo_b = lax.broadcasted_iota(jnp.int32, (ATT_BLOCK, LANES), 1) < HEAD_DIM
    scale = HEAD_DIM ** -0.5
    for blk in range(nblk):
        kk = kbuf[blk * ATT_BLOCK:(blk + 2) * ATT_BLOCK, :]
        vv = vbuf[blk * ATT_BLOCK:(blk + 2) * ATT_BLOCK, :]
        if blk == 0:
            valid = band & ((jj >= ATT_BLOCK) | (si > 0))
        else:
            valid = band
        for c in range(D_ATTN // LANES):
            qc = qcols[c][blk * ATT_BLOCK:(blk + 1) * ATT_BLOCK, :] * scale
            outs = []
            for e in range(2):
                head = HEAD_ORDER[2 * c + e]
                qe = jnp.where(lane_lo_b if e == 0 else ~lane_lo_b, qc, 0.0).astype(jnp.bfloat16)
                s = lax.dot_general(qe, kk, (((1,), (1,)), ((), ())), preferred_element_type=jnp.float32)
                s = s - SLOPES[head] * distf
                s = jnp.where(valid, s, NEG)
                sink = sinks_ref[head]
                m = jnp.maximum(jnp.max(s, axis=-1, keepdims=True), sink)
                ex = jnp.exp(s - m)
                den = jnp.sum(ex, axis=-1, keepdims=True) + jnp.exp(sink - m)
                p = ex / den
                outs.append(jnp.dot(p.astype(jnp.bfloat16), vv, preferred_element_type=jnp.float32))
            att = jnp.where(lane_lo_b, outs[0], outs[1])
            mixbuf[blk * ATT_BLOCK:(blk + 1) * ATT_BLOCK, c * LANES:(c + 1) * LANES] = att.astype(mixbuf.dtype)
        for cc in range(ATT_BLOCK // rows):
            conv_chunk(blk * ATT_BLOCK + cc * rows)

    x1_ref[0] = x + jnp.dot(mixbuf[...], wout_ref[...], preferred_element_type=jnp.float32)

    kbuf[0:ATT_BLOCK, :] = kbuf[T:T + ATT_BLOCK, :]
    vbuf[0:ATT_BLOCK, :] = vbuf[T:T + ATT_BLOCK, :]
    gbuf[0:CONV_PAD, :] = gbuf[T:T + CONV_PAD, :]


def _mixer_prompt(x, sinks, nmix, win, qg, kg, dww, dwb, lng, lnb, wout, tile):
    B, S, D = x.shape
    d_in = win.shape[1]
    d_conv = dww.shape[1]
    d_mix = wout.shape[0]
    full = lambda shape: pl.BlockSpec(shape, lambda b, s: (0,) * len(shape))
    return pl.pallas_call(
        _mixer_prompt_kernel,
        grid=(B, S // tile),
        in_specs=[
            pl.BlockSpec(memory_space=pltpu.SMEM),
            pl.BlockSpec((1, tile, D), lambda b, s: (b, s, 0)),
            full((1, D)), full((D, d_in)), full((1, D_ATTN)), full((1, D_KV)),
            full((CONV_WIDTH, d_conv)), full((1, d_conv)), full((1, d_conv)), full((1, d_conv)),
            full((d_mix, D)),
        ],
        out_specs=[
            pl.BlockSpec((1, tile, D), lambda b, s: (b, s, 0)),
            pl.BlockSpec((1, WINDOW, D_KV), lambda b, s: (b, 0, 0)),
            pl.BlockSpec((1, WINDOW, D_KV), lambda b, s: (b, 0, 0)),
            pl.BlockSpec((1, CONV_STATE, d_conv), lambda b, s: (b, 0, 0)),
        ],
        out_shape=[
            jax.ShapeDtypeStruct((B, S, D), jnp.float32),
            jax.ShapeDtypeStruct((B, WINDOW, D_KV), jnp.float32),
            jax.ShapeDtypeStruct((B, WINDOW, D_KV), jnp.float32),
            jax.ShapeDtypeStruct((B, CONV_STATE, d_conv), jnp.float32),
        ],
        scratch_shapes=[
            pltpu.VMEM((tile + ATT_BLOCK, D_KV), jnp.bfloat16),
            pltpu.VMEM((tile + ATT_BLOCK, D_KV), jnp.bfloat16),
            pltpu.VMEM((tile + CONV_PAD, d_conv), jnp.float32),
            pltpu.VMEM((tile, d_mix), jnp.bfloat16),
        ],
        compiler_params=pltpu.CompilerParams(
            dimension_semantics=("arbitrary", "arbitrary"), vmem_limit_bytes=VMEM_LIMIT),
        name="mixer_prompt",
    )(sinks, x, nmix, win, qg, kg, dww, dwb, lng, lnb, wout)


def _mixer_sample_kernel(sinks_ref, x_ref, ck_ref, cv_ref, st_ref, nmix_ref, win_ref, qg_ref, kg_ref, dww_ref,
                         dwb_ref, lng_ref, lnb_ref, wout_ref, x1_ref, nk_ref, nv_ref, nc_ref,
                         qbuf, knbuf, vnbuf, glubuf, mixbuf):
    nb = x_ref.shape[0]
    d_conv = st_ref.shape[2]
    x = x_ref[...]
    h = _rms(x, nmix_ref[...])
    z = jnp.dot(h.astype(jnp.bfloat16), win_ref[...], preferred_element_type=jnp.float32)
    o1, o2, o3, o4 = D_ATTN, D_ATTN + D_KV, D_ATTN + 2 * D_KV, D_ATTN + 2 * D_KV + d_conv
    lane_lo = lax.broadcasted_iota(jnp.int32, (nb, LANES), 1) < HEAD_DIM
    qcols = _head_rms(z[:, :o1], qg_ref[...], lane_lo)
    kn = _head_rms(z[:, o1:o2], kg_ref[...], lane_lo)[0]
    v = z[:, o2:o3]
    glu = z[:, o3:o4] * jax.nn.sigmoid(z[:, o4:])
    for c in range(D_ATTN // LANES):
        qbuf[:, c * LANES:(c + 1) * LANES] = qcols[c] * (HEAD_DIM ** -0.5)
    knbuf[...] = kn
    vnbuf[...] = v
    glubuf[...] = glu

    row_lo = lax.broadcasted_iota(jnp.int32, (1, LANES), 1) < HEAD_DIM
    hrow = lax.broadcasted_iota(jnp.int32, (N_Q_HEADS, LANES), 0)
    hlane = lax.broadcasted_iota(jnp.int32, (N_Q_HEADS, LANES), 1)
    keep = (hrow % 2) == (hlane // HEAD_DIM)
    jcol = lax.broadcasted_iota(jnp.int32, (N_Q_HEADS, WINDOW), 1)
    distf = (WINDOW - jcol).astype(jnp.float32)
    hcol = lax.broadcasted_iota(jnp.int32, (N_Q_HEADS, 1), 0)
    slope = jnp.zeros((N_Q_HEADS, 1), jnp.float32)
    sink = jnp.zeros((N_Q_HEADS, 1), jnp.float32)
    for r in range(N_Q_HEADS):
        slope = jnp.where(hcol == r, SLOPES[HEAD_ORDER[r]], slope)
        sink = jnp.where(hcol == r, sinks_ref[HEAD_ORDER[r]], sink)
    bias = slope * distf
    dw_hist = dww_ref[0:CONV_STATE, :]
    dw_last = dww_ref[CONV_STATE:CONV_WIDTH, :]

    def one_seq(b):
        qrow = qbuf[pl.ds(b, 1), :]
        qp = jnp.concatenate(
            [jnp.broadcast_to(qrow[:, c * LANES:(c + 1) * LANES], (2, LANES)) for c in range(D_ATTN // LANES)], axis=0)
        qp = jnp.where(keep, qp, 0.0)
        ckb = ck_ref[b]
        cvb = cv_ref[b]
        knr = knbuf[pl.ds(b, 1), :]
        vnr = vnbuf[pl.ds(b, 1), :]
        glur = glubuf[pl.ds(b, 1), :]
        stb = st_ref[b]

        nk_ref[b, 0:WINDOW - 1, :] = ckb[1:WINDOW, :]
        nk_ref[b, WINDOW - 1:WINDOW, :] = knr
        nv_ref[b, 0:WINDOW - 1, :] = cvb[1:WINDOW, :]
        nv_ref[b, WINDOW - 1:WINDOW, :] = vnr
        nc_ref[b, 0:CONV_STATE - 1, :] = stb[1:CONV_STATE, :]
        nc_ref[b, CONV_STATE - 1:CONV_STATE, :] = glur

        s = lax.dot_general(qp.astype(jnp.bfloat16), ckb.astype(jnp.bfloat16), (((1,), (1,)), ((), ())),
                            preferred_element_type=jnp.float32) - bias
        s_new = jnp.sum(qp * knr, axis=-1, keepdims=True)
        m = jnp.maximum(jnp.maximum(jnp.max(s, axis=-1, keepdims=True), s_new), sink)
        ex = jnp.exp(s - m)
        ex_new = jnp.exp(s_new - m)
        den = jnp.sum(ex, axis=-1, keepdims=True) + ex_new + jnp.exp(sink - m)
        o = jnp.dot((ex / den).astype(jnp.bfloat16), cvb.astype(jnp.bfloat16), preferred_element_type=jnp.float32)
        o = o + (ex_new / den) * vnr
        att = [jnp.where(row_lo, o[2 * c:2 * c + 1, :], o[2 * c + 1:2 * c + 2, :]) for c in range(D_ATTN // LANES)]
        conv = jnp.sum(stb * dw_hist, axis=0, keepdims=True) + glur * dw_last
        return jnp.concatenate(att + [conv], axis=1)

    def per_group(gi, carry):
        base = pl.multiple_of(gi * SUBLANES, SUBLANES)
        mixbuf[pl.ds(base, SUBLANES), :] = jnp.concatenate([one_seq(base + u) for u in range(SUBLANES)], axis=0)
        return carry

    lax.fori_loop(0, nb // SUBLANES, per_group, 0)

    cvo = _layernorm_silu(mixbuf[:, D_ATTN:] + dwb_ref[...], lng_ref[...], lnb_ref[...])
    mixbuf[:, D_ATTN:] = cvo
    x1_ref[...] = x + jnp.dot(mixbuf[...].astype(jnp.bfloat16), wout_ref[...], preferred_element_type=jnp.float32)


def _mixer_sample(x, ck, cv, st, sinks, nmix, win, qg, kg, dww, dwb, lng, lnb, wout, nb):
    DB, D = x.shape
    d_in = win.shape[1]
    d_conv = dww.shape[1]
    d_mix = wout.shape[0]
    full = lambda shape: pl.BlockSpec(shape, lambda i: (0,) * len(shape))
    return pl.pallas_call(
        _mixer_sample_kernel,
        grid=(DB // nb,),
        in_specs=[
            pl.BlockSpec(memory_space=pltpu.SMEM),
            pl.BlockSpec((nb, D), lambda i: (i, 0)),
            pl.BlockSpec((nb, WINDOW, D_KV), lambda i: (i, 0, 0)),
            pl.BlockSpec((nb, WINDOW, D_KV), lambda i: (i, 0, 0)),
            pl.BlockSpec((nb, CONV_STATE, d_conv), lambda i: (i, 0, 0)),
            full((1, D)), full((D, d_in)), full((1, D_ATTN)), full((1, D_KV)),
            full((CONV_WIDTH, d_conv)), full((1, d_conv)), full((1, d_conv)), full((1, d_conv)),
            full((d_mix, D)),
        ],
        out_specs=[
            pl.BlockSpec((nb, D), lambda i: (i, 0)),
            pl.BlockSpec((nb, WINDOW, D_KV), lambda i: (i, 0, 0)),
            pl.BlockSpec((nb, WINDOW, D_KV), lambda i: (i, 0, 0)),
            pl.BlockSpec((nb, CONV_STATE, d_conv), lambda i: (i, 0, 0)),
        ],
        out_shape=[
            jax.ShapeDtypeStruct((DB, D), jnp.float32),
            jax.ShapeDtypeStruct((DB, WINDOW, D_KV), jnp.float32),
            jax.ShapeDtypeStruct((DB, WINDOW, D_KV), jnp.float32),
            jax.ShapeDtypeStruct((DB, CONV_STATE, d_conv), jnp.float32),
        ],
        scratch_shapes=[
            pltpu.VMEM((nb, D_ATTN), jnp.float32),
            pltpu.VMEM((nb, D_KV), jnp.float32),
            pltpu.VMEM((nb, D_KV), jnp.float32),
            pltpu.VMEM((nb, d_conv), jnp.float32),
            pltpu.VMEM((nb, d_mix), jnp.float32),
        ],
        compiler_params=pltpu.CompilerParams(dimension_semantics=("arbitrary",), vmem_limit_bytes=VMEM_LIMIT),
        name="mixer_sample",
    )(sinks, x, ck, cv, st, nmix, win, qg, kg, dww, dwb, lng, lnb, wout)


GROUP = SUBLANES
N_SLOTS = 2 * GROUP
DMA_THREADS = 2
BITREV = [0, 4, 2, 6, 1, 5, 3, 7]
PEER_TILE = 2 * GROUP * PEER_HEADS
ROUTE_SLICES = 2 * GROUP


def _sublane_sums(ps, sub):
    def merge(a, b, keep_a, shift):
        x = jnp.where(keep_a, a, pltpu.roll(b, shift, 0))
        y = jnp.where(keep_a, pltpu.roll(a, SUBLANES - shift, 0), b)
        return x + y

    z = [merge(ps[2 * i], ps[2 * i + 1], sub < 4, 4) for i in range(4)]
    w = [merge(z[2 * i], z[2 * i + 1], (sub % 4) < 2, 2) for i in range(2)]
    return merge(w[0], w[1], (sub % 2) < 1, 1)


def _peer_kernel(x1c_ref, x1ahead_ref, x1first_ref, x1second_ref, nffnc_ref, nffn_ref, wqT_ref, sk_ref, cpos_ref,
                 slot_like_ref, tab_ref, out_ref, buf, h2buf, qT, sa_s, sb_s, va_s, vb_s, ia_s, ib_s, cand_s, cid_s,
                 tops_s, eids_s, eid_hk, ids_v, ids_s, gates, sem, ids_sem):
    step = pl.program_id(0)
    nsteps = pl.num_programs(0)
    T = x1c_ref.shape[0]
    rows = PEER_HEADS * PEER_TOPK
    n_chunks = x1c_ref.shape[1]
    d = n_chunks * x1c_ref.shape[2]
    cur = step % 3
    nxt = (step + 1) % 3
    ahead = (step + 2) % 3

    key_id = lax.broadcasted_iota(jnp.int32, (N_KEYS, LANES), 0)
    neg_pad = jnp.full((N_CAND_PAD - N_CAND, LANES), -jnp.inf, jnp.float32)
    zero_pad = jnp.zeros((N_CAND_PAD - N_CAND, LANES), jnp.int32)

    def route_project(x_ref):
        h2 = _rms(x_ref[...], nffn_ref[...])
        qT[...] = lax.dot_general(wqT_ref[...], h2.astype(jnp.bfloat16), (((1,), (1,)), ((), ())),
                                  preferred_element_type=jnp.float32).astype(qT.dtype)

    def key_scores(hh):
        qb = qT[pl.ds(pl.multiple_of(hh * 2 * D_HALF, 2 * D_HALF), 2 * D_HALF), :]
        sa_s[...] = jnp.dot(sk_ref[0], qb[:D_HALF], preferred_element_type=jnp.float32)
        sb_s[...] = jnp.dot(sk_ref[1], qb[D_HALF:], preferred_element_type=jnp.float32)

    def key_top_steps(it0, n):
        halves = [(sa_s, va_s, ia_s), (sb_s, vb_s, ib_s)]
        vals = [ref[...] for ref, _, _ in halves]
        for it in range(it0, it0 + n):
            for k, (_, v_ref, i_ref) in enumerate(halves):
                s = vals[k]
                m = jnp.max(s, axis=0, keepdims=True)
                im = jnp.min(jnp.where(s == m, key_id, N_KEYS), axis=0, keepdims=True)
                v_ref[it:it + 1, :] = m
                i_ref[it:it + 1, :] = im
                vals[k] = jnp.where(key_id == im, -jnp.inf, s)
        for k, (ref, _, _) in enumerate(halves):
            ref[...] = vals[k]

    def cand_build():
        va, vb, ia, ib = va_s[...], vb_s[...], ia_s[...], ib_s[...]
        cand, cid = [], []
        for a in range(PEER_TOPK):
            nbb = PEER_TOPK // (a + 1)
            cand.append(va[a:a + 1] + vb[0:nbb])
            cid.append(ia[a:a + 1] * N_KEYS + ib[0:nbb])
        cand_s[...] = jnp.concatenate(cand + [neg_pad], axis=0)
        cid_s[...] = jnp.concatenate(cid + [zero_pad], axis=0)

    def cand_top_steps(it0, n):
        cand = cand_s[...]
        cid = cid_s[...]
        cpos = cpos_ref[...]
        for it in range(it0, it0 + n):
            m = jnp.max(cand, axis=0, keepdims=True)
            pm = jnp.min(jnp.where(cand == m, cpos, PEER_TOPK * PEER_TOPK), axis=0, keepdims=True)
            hit = cpos == pm
            tops_s[it:it + 1, :] = m
            eids_s[it:it + 1, :] = jnp.max(jnp.where(hit, cid, -1), axis=0, keepdims=True)
            cand = jnp.where(hit, -jnp.inf, cand)
        cand_s[...] = cand

    def head_finish(hh, dst):
        ts = tops_s[...]
        ex = jnp.exp(ts - ts[0:1])
        orow = pl.ds(pl.multiple_of(hh * PEER_TOPK, PEER_TOPK), PEER_TOPK)
        eid_hk[orow, :] = eids_s[...]
        gates[dst, orow, :] = ex / jnp.sum(ex, axis=0, keepdims=True)

    per_slice = 2 * PEER_TOPK // ROUTE_SLICES

    def route_slice(hh, r, dst):
        half = ROUTE_SLICES // 2
        if r == 0:
            key_scores(hh)
        if r < half:
            key_top_steps(r * per_slice, per_slice)
        if r == half:
            cand_build()
        if r >= half:
            cand_top_steps((r - half) * per_slice, per_slice)
        if r == ROUTE_SLICES - 1:
            head_finish(hh, dst)

    def ids_copy(dst):
        return pltpu.make_async_copy(ids_v, ids_s.at[dst], ids_sem)

    def route_publish(dst):
        ids_v[...] = eid_hk[...].T
        ids_copy(dst).start()

    def issue_token(src, tok, slot):
        for j in range(rows):
            pltpu.make_async_copy(tab_ref.at[ids_s[src, tok, j]], buf.at[slot, j // SUBLANES, :, j % SUBLANES],
                                  sem.at[slot]).start(priority=j % DMA_THREADS)

    def wait_token(slot):
        pltpu.make_async_copy(slot_like_ref, buf.at[slot], sem.at[slot]).wait()

    lane_id = lax.broadcasted_iota(jnp.int32, (rows, T), 1)
    sub = lax.broadcasted_iota(jnp.int32, (SUBLANES, LANES), 0)
    inv_sqrt2 = 1.0 / math.sqrt(2.0)

    def compute_token(t, slot):
        h = h2buf[t]
        gcol = jnp.sum(jnp.where(lane_id == t, gates[cur], 0.0), axis=1, keepdims=True)
        hq = [jnp.broadcast_to(h[q:q + 1, :], (SUBLANES, LANES)) for q in range(n_chunks)]
        acc = [jnp.zeros((SUBLANES, LANES), jnp.float32) for _ in range(n_chunks)]
        for jg in range(rows // SUBLANES):
            r0 = jg * SUBLANES
            prod = [buf[slot, jg, q] * hq[q] for q in range(n_chunks)]
            while len(prod) > 1:
                prod = [prod[k] + prod[k + 1] for k in range(0, len(prod), 2)]
            sc = jnp.sum(prod[0], axis=1, keepdims=True)
            act = 0.5 * sc * (1.0 + lax.erf(sc * inv_sqrt2))
            wb = jnp.broadcast_to(gcol[r0:r0 + SUBLANES] * act, (SUBLANES, LANES))
            for q in range(n_chunks):
                acc[q] = acc[q] + buf[slot, jg, n_chunks + q] * wb
        out_ref[t] = _sublane_sums([acc[BITREV[k]] for k in range(SUBLANES)], sub)

    def run_group(first_tok, par, issue_next, head):
        for u in range(GROUP):
            if issue_next is not None:
                issue_next(u, (1 - par) * GROUP + u)
            slot = par * GROUP + u
            wait_token(slot)
            compute_token(first_tok + u, slot)
            route_slice(head, par * GROUP + u, ahead)

    @pl.when(step == 0)
    def _():
        for x_ref, dst in ((x1first_ref, 0), (x1second_ref, 1)):
            route_project(x_ref)

            def head(hh, carry):
                for r in range(ROUTE_SLICES):
                    route_slice(hh, r, dst)
                return carry

            lax.fori_loop(0, PEER_HEADS, head, 0)
            route_publish(dst)
            ids_copy(dst).wait()
        for u in range(GROUP):
            issue_token(0, u, u)

    x = x1c_ref[...]
    ms = jnp.sum(jnp.sum(x * x, axis=2, keepdims=True), axis=1, keepdims=True) * (1.0 / d)
    h2buf[...] = x * lax.rsqrt(ms + EPS) * nffnc_ref[...]
    route_project(x1ahead_ref)

    def first_group(first_tok, head):
        run_group(first_tok, 0, lambda u, slot: issue_token(cur, first_tok + GROUP + u, slot), head)

    def body(pi, carry):
        first_tok = pi * 2 * GROUP
        first_group(first_tok, pi)
        run_group(first_tok + GROUP, 1, lambda u, slot: issue_token(cur, first_tok + 2 * GROUP + u, slot), pi)
        return carry

    lax.fori_loop(0, PEER_HEADS - 1, body, 0)

    last_tok = (PEER_HEADS - 1) * 2 * GROUP
    first_group(last_tok, PEER_HEADS - 1)

    @pl.when(step + 1 < nsteps)
    def _():
        for u in range(GROUP):
            issue_token(nxt, u, u)

    run_group(last_tok + GROUP, 1, None, PEER_HEADS - 1)
    route_publish(ahead)
    ids_copy(ahead).wait()


def _peer(x1, nffn, wqT, sk, cpos, tab):
    N, D = x1.shape
    T = PEER_TILE
    rows = PEER_HEADS * PEER_TOPK
    chunks = D // LANES
    nq = wqT.shape[0]
    assert chunks == SUBLANES and N % T == 0 and T == LANES
    nsteps = N // T
    slot_shape = (rows // SUBLANES, 2 * chunks, SUBLANES, LANES)
    slot_like = jnp.zeros(slot_shape, jnp.float32)
    full = lambda shape: pl.BlockSpec(shape, lambda i: (0,) * len(shape))
    out = pl.pallas_call(
        _peer_kernel,
        grid=(nsteps,),
        in_specs=[
            pl.BlockSpec((T, chunks, LANES), lambda i: (i, 0, 0)),
            pl.BlockSpec((T, D), lambda i: (jnp.minimum(i + 2, nsteps - 1), 0)),
            pl.BlockSpec((T, D), lambda i: (0, 0)),
            pl.BlockSpec((T, D), lambda i: (min(1, nsteps - 1), 0)),
            full((chunks, LANES)), full((1, D)), full((nq, D)), full((2, N_KEYS, D_HALF)), full((N_CAND_PAD, LANES)),
            pl.BlockSpec(memory_space=pl.ANY),
            pl.BlockSpec(memory_space=pl.ANY),
        ],
        out_specs=pl.BlockSpec((T, chunks, LANES), lambda i: (i, 0, 0)),
        out_shape=jax.ShapeDtypeStruct((N, chunks, LANES), jnp.float32),
        scratch_shapes=[
            pltpu.VMEM((N_SLOTS,) + slot_shape, jnp.float32),
            pltpu.VMEM((T, chunks, LANES), jnp.float32),
            pltpu.VMEM((nq, T), jnp.bfloat16),
            pltpu.VMEM((N_KEYS, T), jnp.float32), pltpu.VMEM((N_KEYS, T), jnp.float32),
            pltpu.VMEM((PEER_TOPK, T), jnp.float32), pltpu.VMEM((PEER_TOPK, T), jnp.float32),
            pltpu.VMEM((PEER_TOPK, T), jnp.int32), pltpu.VMEM((PEER_TOPK, T), jnp.int32),
            pltpu.VMEM((N_CAND_PAD, T), jnp.float32), pltpu.VMEM((N_CAND_PAD, T), jnp.int32),
            pltpu.VMEM((PEER_TOPK, T), jnp.float32), pltpu.VMEM((PEER_TOPK, T), jnp.int32),
            pltpu.VMEM((rows, T), jnp.int32),
            pltpu.VMEM((T, rows), jnp.int32),
            pltpu.SMEM((3, T, rows), jnp.int32),
            pltpu.VMEM((3, rows, T), jnp.float32),
            pltpu.SemaphoreType.DMA((N_SLOTS,)),
            pltpu.SemaphoreType.DMA,
        ],
        compiler_params=pltpu.CompilerParams(dimension_semantics=("arbitrary",), vmem_limit_bytes=VMEM_LIMIT),
        name="peer",
    )(x1.reshape(N, chunks, LANES), x1, x1, x1, nffn.reshape(chunks, LANES), nffn, wqT, sk, cpos, slot_like, tab)
    return out.reshape(N, D)


def _finish_kernel(x1_ref, peer_ref, p_ref, nple_ref, wgate_ref, wproj_ref, y_ref):
    x2 = x1_ref[...] + peer_ref[...]
    hp = _rms(x2, nple_ref[...])
    gate = jax.nn.sigmoid(jnp.dot(hp.astype(jnp.bfloat16), wgate_ref[...], preferred_element_type=jnp.float32))
    proj = jnp.dot(p_ref[...].astype(jnp.bfloat16), wproj_ref[...], preferred_element_type=jnp.float32)
    y_ref[...] = x2 + proj * gate


def _finish(x1, peer, p, nple, wgate, wproj, tile):
    N, D = x1.shape
    dp = p.shape[1]
    full = lambda shape: pl.BlockSpec(shape, lambda i: (0,) * len(shape))
    return pl.pallas_call(
        _finish_kernel,
        grid=(N // tile,),
        in_specs=[
            pl.BlockSpec((tile, D), lambda i: (i, 0)),
            pl.BlockSpec((tile, D), lambda i: (i, 0)),
            pl.BlockSpec((tile, dp), lambda i: (i, 0)),
            full((1, D)), full((D, D)), full((dp, D)),
        ],
        out_specs=pl.BlockSpec((tile, D), lambda i: (i, 0)),
        out_shape=jax.ShapeDtypeStruct((N, D), jnp.float32),
        compiler_params=pltpu.CompilerParams(dimension_semantics=("arbitrary",), vmem_limit_bytes=VMEM_LIMIT),
        name="finish",
    )(x1, peer, p, nple, wgate, wproj)


def _tile_for(n, pref):
    t = min(pref, n)
    while n % t:
        t //= 2
    return t


def kernel(x_prompt, x_sample, cache_k, cache_v, state_conv, p_prompt, p_sample, norm_mix, w_in, q_gain, k_gain,
           attn_sinks, conv_dw_w, conv_dw_b, conv_ln_g, conv_ln_b, w_out, norm_ffn, peer_wq, peer_sub_keys, peer_u,
           peer_v, norm_ple, w_ple_proj, w_ple_gate):
    depth = norm_mix.shape[0]
    assert depth == 1
    B, S, D = x_prompt.shape
    DB, dec_seq, _ = x_sample.shape
    assert dec_seq == 1
    bf = jnp.bfloat16
    i = 0

    qperm = np.concatenate([np.arange(h * HEAD_DIM, (h + 1) * HEAD_DIM) for h in HEAD_ORDER])
    win = jnp.concatenate([w_in[i][:, qperm], w_in[i][:, D_ATTN:]], axis=1).astype(bf)
    wout = jnp.concatenate([w_out[i][qperm], w_out[i][D_ATTN:]], axis=0).astype(bf)
    qg = jnp.tile(q_gain[i], N_Q_HEADS)[None]
    kg = jnp.tile(k_gain[i], N_KV_HEADS)[None]
    sinks = attn_sinks[i]
    nmix = norm_mix[i][None]
    dww, dwb, lng, lnb = conv_dw_w[i], conv_dw_b[i][None], conv_ln_g[i][None], conv_ln_b[i][None]
    nffn = norm_ffn[i][None]
    wqT = peer_wq[i].T.astype(bf)
    sk = peer_sub_keys[i].astype(bf)
    n_exp = peer_u.shape[1]
    tab = jnp.concatenate([peer_u[i].reshape(n_exp, D // LANES, LANES), peer_v[i].reshape(n_exp, D // LANES, LANES)],
                          axis=1)
    nple = norm_ple[i][None]
    wgate = w_ple_gate[i].astype(bf)
    wproj = w_ple_proj[i].astype(bf)
    cpos = np.full((N_CAND_PAD, LANES), PEER_TOPK * PEER_TOPK, np.int32)
    for r, (a, b) in enumerate(CAND_AB):
        cpos[r, :] = a * PEER_TOPK + b
    cpos = jnp.asarray(cpos)

    def tail(x1, p):
        n = x1.shape[0]
        peer = _peer(x1, nffn, wqT, sk, cpos, tab)
        return _finish(x1, peer, p, nple, wgate, wproj, _tile_for(n, 512))

    x1p, nkp, nvp, ncp = _mixer_prompt(x_prompt, sinks, nmix, win, qg, kg, dww, dwb, lng, lnb, wout, _tile_for(S, 512))
    yp = tail(x1p.reshape(B * S, D), p_prompt[i].reshape(B * S, -1)).reshape(B, S, D)

    ck = cache_k[i].reshape(DB, WINDOW, D_KV)
    cv = cache_v[i].reshape(DB, WINDOW, D_KV)
    x1s, nks, nvs, ncs = _mixer_sample(x_sample.reshape(DB, D), ck, cv, state_conv[i], sinks, nmix, win, qg, kg,
                                       dww, dwb, lng, lnb, wout, _tile_for(DB, 32))
    ys = tail(x1s, p_sample[i].reshape(DB, -1)).reshape(DB, 1, D)

    kv5 = lambda a: a.reshape(1, a.shape[0], WINDOW, N_KV_HEADS, HEAD_DIM)
    return (yp, ys, kv5(nkp), kv5(nvp), ncp[None], kv5(nks), kv5(nvs), ncs[None])
```

```python
import functools
import math

import numpy as np
import jax
import jax.numpy as jnp
from jax import lax
from jax.experimental import pallas as pl
from jax.experimental.pallas import tpu as pltpu

HEAD_DIM = 64
N_Q_HEADS = 8
N_KV_HEADS = 2
Q_PER_KV = N_Q_HEADS // N_KV_HEADS
D_ATTN = N_Q_HEADS * HEAD_DIM
D_KV = N_KV_HEADS * HEAD_DIM
WINDOW = 128
ATT_BLOCK = 128
CONV_WIDTH = 31
CONV_STATE = CONV_WIDTH - 1
N_KEYS = 128
PEER_HEADS = 8
PEER_TOPK = 16
D_HALF = 128
EPS = 1e-6
NEG = -1e30
LANES = 128
SUBLANES = 8
CONV_PAD = 32
VMEM_LIMIT = 56 * 1024 * 1024

HEAD_ORDER = [0, 4, 1, 5, 2, 6, 3, 7]
SLOPES = [2.0 ** (-8.0 * (h + 1) / N_Q_HEADS) for h in range(N_Q_HEADS)]

CAND_AB = [(a, b) for a in range(PEER_TOPK) for b in range(PEER_TOPK // (a + 1))]
N_CAND = len(CAND_AB)
N_CAND_PAD = -(-N_CAND // 8) * 8


def _rms(x, g):
    return x * lax.rsqrt(jnp.mean(x * x, axis=-1, keepdims=True) + EPS) * g


def _head_rms(a, gain, lane_lo):
    cols = []
    for c in range(a.shape[1] // LANES):
        ac = a[:, c * LANES:(c + 1) * LANES]
        sq = ac * ac
        lo = jnp.sum(jnp.where(lane_lo, sq, 0.0), axis=-1, keepdims=True)
        hi = jnp.sum(jnp.where(lane_lo, 0.0, sq), axis=-1, keepdims=True)
        ms = jnp.where(lane_lo, lo, hi) * (1.0 / HEAD_DIM)
        cols.append(ac * lax.rsqrt(ms + EPS) * gain[:, c * LANES:(c + 1) * LANES])
    return cols


def _layernorm_silu(y, g, b):
    mu = jnp.mean(y, axis=-1, keepdims=True)
    yc = y - mu
    yn = yc * lax.rsqrt(jnp.mean(yc * yc, axis=-1, keepdims=True) + EPS) * g + b
    return yn * jax.nn.sigmoid(yn)


def _mixer_prompt_kernel(sinks_ref, x_ref, nmix_ref, win_ref, qg_ref, kg_ref, dww_ref, dwb_ref, lng_ref, lnb_ref,
                         wout_ref, x1_ref, nk_ref, nv_ref, nc_ref, kbuf, vbuf, gbuf, mixbuf):
    si = pl.program_id(1)
    T = x_ref.shape[1]
    d_conv = gbuf.shape[1]
    nblk = T // ATT_BLOCK

    @pl.when(si == 0)
    def _():
        kbuf[0:ATT_BLOCK, :] = jnp.zeros((ATT_BLOCK, D_KV), kbuf.dtype)
        vbuf[0:ATT_BLOCK, :] = jnp.zeros((ATT_BLOCK, D_KV), vbuf.dtype)
        gbuf[0:CONV_PAD, :] = jnp.zeros((CONV_PAD, d_conv), gbuf.dtype)

    x = x_ref[0]
    h = _rms(x, nmix_ref[...])
    z = jnp.dot(h.astype(jnp.bfloat16), win_ref[...], preferred_element_type=jnp.float32)
    o1, o2, o3, o4 = D_ATTN, D_ATTN + D_KV, D_ATTN + 2 * D_KV, D_ATTN + 2 * D_KV + d_conv
    lane_lo = lax.broadcasted_iota(jnp.int32, (T, LANES), 1) < HEAD_DIM

    qcols = _head_rms(z[:, :o1], qg_ref[...], lane_lo)
    kn = _head_rms(z[:, o1:o2], kg_ref[...], lane_lo)[0]
    v = z[:, o2:o3]
    glu = z[:, o3:o4] * jax.nn.sigmoid(z[:, o4:])

    kbuf[ATT_BLOCK:ATT_BLOCK + T, :] = kn.astype(kbuf.dtype)
    vbuf[ATT_BLOCK:ATT_BLOCK + T, :] = v.astype(vbuf.dtype)
    gbuf[CONV_PAD:CONV_PAD + T, :] = glu

    @pl.when(si == pl.num_programs(1) - 1)
    def _():
        nk_ref[0] = kn[T - WINDOW:, :]
        nv_ref[0] = v[T - WINDOW:, :]
        nc_ref[0] = glu[T - CONV_STATE:, :]

    rows = 32
    dwb = dwb_ref[...]
    lng = lng_ref[...]
    lnb = lnb_ref[...]
    first = CONV_PAD - CONV_STATE

    def conv_chunk(base):
        acc = jnp.zeros((rows, d_conv), jnp.float32)
        win = gbuf[base:base + rows + CONV_PAD, :]
        for r in range(SUBLANES):
            span = (rows + CONV_PAD - r) // SUBLANES * SUBLANES
            shifted = win[r:r + span, :]
            for w in range(CONV_WIDTH):
                if (first + w) % SUBLANES == r:
                    k = first + w - r
                    acc = acc + shifted[k:k + rows, :] * dww_ref[w:w + 1, :]
        cv = _layernorm_silu(acc + dwb, lng, lnb)
        mixbuf[base:base + rows, D_ATTN:] = cv.astype(mixbuf.dtype)

    ii = lax.broadcasted_iota(jnp.int32, (ATT_BLOCK, 2 * ATT_BLOCK), 0)
    jj = lax.broadcasted_iota(jnp.int32, (ATT_BLOCK, 2 * ATT_BLOCK), 1)
    dist = ATT_BLOCK + ii - jj
    band = (dist >= 0) & (dist <= WINDOW)
    distf = dist.astype(jnp.float32)
    lane_lo_b = lax.broadcasted_iota(jnp.int32, (ATT_BLOCK, LANES), 1) < HEAD_DIM
    scale = HEAD_DIM ** -0.5
    for blk in range(nblk):
        kk = kbuf[blk * ATT_BLOCK:(blk + 2) * ATT_BLOCK, :]
        vv = vbuf[blk * ATT_BLOCK:(blk + 2) * ATT_BLOCK, :]
        if blk == 0:
            valid = band & ((jj >= ATT_BLOCK) | (si > 0))
        else:
            valid = band
        for c in range(D_ATTN // LANES):
            qc = qcols[c][blk * ATT_BLOCK:(blk + 1) * ATT_BLOCK, :] * scale
            outs = []
            for e in range(2):
                head = HEAD_ORDER[2 * c + e]
                qe = jnp.where(lane_lo_b if e == 0 else ~lane_lo_b, qc, 0.0).astype(jnp.bfloat16)
                s = lax.dot_general(qe, kk, (((1,), (1,)), ((), ())), preferred_element_type=jnp.float32)
                s = s - SLOPES[head] * distf
                s = jnp.where(valid, s, NEG)
                sink = sinks_ref[head]
                m = jnp.maximum(jnp.max(s, axis=-1, keepdims=True), sink)
                ex = jnp.exp(s - m)
                den = jnp.sum(ex, axis=-1, keepdims=True) + jnp.exp(sink - m)
                p = ex / den
                outs.append(jnp.dot(p.astype(jnp.bfloat16), vv, preferred_element_type=jnp.float32))
            att = jnp.where(lane_lo_b, outs[0], outs[1])
            mixbuf[blk * ATT_BLOCK:(blk + 1) * ATT_BLOCK, c * LANES:(c + 1) * LANES] = att.astype(mixbuf.dtype)
        for cc in range(ATT_BLOCK // rows):
            conv_chunk(blk * ATT_BLOCK + cc * rows)

    x1_ref[0] = x + jnp.dot(mixbuf[...], wout_ref[...], preferred_element_type=jnp.float32)

    kbuf[0:ATT_BLOCK, :] = kbuf[T:T + ATT_BLOCK, :]
    vbuf[0:ATT_BLOCK, :] = vbuf[T:T + ATT_BLOCK, :]
    gbuf[0:CONV_PAD, :] = gbuf[T:T + CONV_PAD, :]


def _mixer_prompt(x, sinks, nmix, win, qg, kg, dww, dwb, lng, lnb, wout, tile):
    B, S, D = x.shape
    d_in = win.shape[1]
    d_conv = dww.shape[1]
    d_mix = wout.shape[0]
    full = lambda shape: pl.BlockSpec(shape, lambda b, s: (0,) * len(shape))
    return pl.pallas_call(
        _mixer_prompt_kernel,
        grid=(B, S // tile),
        in_specs=[
            pl.BlockSpec(memory_space=pltpu.SMEM),
            pl.BlockSpec((1, tile, D), lambda b, s: (b, s, 0)),
            full((1, D)), full((D, d_in)), full((1, D_ATTN)), full((1, D_KV)),
            full((CONV_WIDTH, d_conv)), full((1, d_conv)), full((1, d_conv)), full((1, d_conv)),
            full((d_mix, D)),
        ],
        out_specs=[
            pl.BlockSpec((1, tile, D), lambda b, s: (b, s, 0)),
            pl.BlockSpec((1, WINDOW, D_KV), lambda b, s: (b, 0, 0)),
            pl.BlockSpec((1, WINDOW, D_KV), lambda b, s: (b, 0, 0)),
            pl.BlockSpec((1, CONV_STATE, d_conv), lambda b, s: (b, 0, 0)),
        ],
        out_shape=[
            jax.ShapeDtypeStruct((B, S, D), jnp.float32),
            jax.ShapeDtypeStruct((B, WINDOW, D_KV), jnp.float32),
            jax.ShapeDtypeStruct((B, WINDOW, D_KV), jnp.float32),
            jax.ShapeDtypeStruct((B, CONV_STATE, d_conv), jnp.float32),
        ],
        scratch_shapes=[
            pltpu.VMEM((tile + ATT_BLOCK, D_KV), jnp.bfloat16),
            pltpu.VMEM((tile + ATT_BLOCK, D_KV), jnp.bfloat16),
            pltpu.VMEM((tile + CONV_PAD, d_conv), jnp.float32),
            pltpu.VMEM((tile, d_mix), jnp.bfloat16),
        ],
        compiler_params=pltpu.CompilerParams(
            dimension_semantics=("arbitrary", "arbitrary"), vmem_limit_bytes=VMEM_LIMIT),
        name="mixer_prompt",
    )(sinks, x, nmix, win, qg, kg, dww, dwb, lng, lnb, wout)


def _mixer_sample_kernel(sinks_ref, x_ref, ck_ref, cv_ref, st_ref, nmix_ref, win_ref, qg_ref, kg_ref, dww_ref,
                         dwb_ref, lng_ref, lnb_ref, wout_ref, x1_ref, nk_ref, nv_ref, nc_ref,
                         qbuf, knbuf, vnbuf, glubuf, mixbuf):
    nb = x_ref.shape[0]
    d_conv = st_ref.shape[2]
    x = x_ref[...]
    h = _rms(x, nmix_ref[...])
    z = jnp.dot(h.astype(jnp.bfloat16), win_ref[...], preferred_element_type=jnp.float32)
    o1, o2, o3, o4 = D_ATTN, D_ATTN + D_KV, D_ATTN + 2 * D_KV, D_ATTN + 2 * D_KV + d_conv
    lane_lo = lax.broadcasted_iota(jnp.int32, (nb, LANES), 1) < HEAD_DIM
    qcols = _head_rms(z[:, :o1], qg_ref[...], lane_lo)
    kn = _head_rms(z[:, o1:o2], kg_ref[...], lane_lo)[0]
    v = z[:, o2:o3]
    glu = z[:, o3:o4] * jax.nn.sigmoid(z[:, o4:])
    for c in range(D_ATTN // LANES):
        qbuf[:, c * LANES:(c + 1) * LANES] = qcols[c] * (HEAD_DIM ** -0.5)
    knbuf[...] = kn
    vnbuf[...] = v
    glubuf[...] = glu

    row_lo = lax.broadcasted_iota(jnp.int32, (1, LANES), 1) < HEAD_DIM
    hrow = lax.broadcasted_iota(jnp.int32, (N_Q_HEADS, LANES), 0)
    hlane = lax.broadcasted_iota(jnp.int32, (N_Q_HEADS, LANES), 1)
    keep = (hrow % 2) == (hlane // HEAD_DIM)
    jcol = lax.broadcasted_iota(jnp.int32, (N_Q_HEADS, WINDOW), 1)
    distf = (WINDOW - jcol).astype(jnp.float32)
    hcol = lax.broadcasted_iota(jnp.int32, (N_Q_HEADS, 1), 0)
    slope = jnp.zeros((N_Q_HEADS, 1), jnp.float32)
    sink = jnp.zeros((N_Q_HEADS, 1), jnp.float32)
    for r in range(N_Q_HEADS):
        slope = jnp.where(hcol == r, SLOPES[HEAD_ORDER[r]], slope)
        sink = jnp.where(hcol == r, sinks_ref[HEAD_ORDER[r]], sink)
    bias = slope * distf
    dw_hist = dww_ref[0:CONV_STATE, :]
    dw_last = dww_ref[CONV_STATE:CONV_WIDTH, :]

    def one_seq(b):
        qrow = qbuf[pl.ds(b, 1), :]
        qp = jnp.concatenate(
            [jnp.broadcast_to(qrow[:, c * LANES:(c + 1) * LANES], (2, LANES)) for c in range(D_ATTN // LANES)], axis=0)
        qp = jnp.where(keep, qp, 0.0)
        ckb = ck_ref[b]
        cvb = cv_ref[b]
        knr = knbuf[pl.ds(b, 1), :]
        vnr = vnbuf[pl.ds(b, 1), :]
        glur = glubuf[pl.ds(b, 1), :]
        stb = st_ref[b]

        nk_ref[b, 0:WINDOW - 1, :] = ckb[1:WINDOW, :]
        nk_ref[b, WINDOW - 1:WINDOW, :] = knr
        nv_ref[b, 0:WINDOW - 1, :] = cvb[1:WINDOW, :]
        nv_ref[b, WINDOW - 1:WINDOW, :] = vnr
        nc_ref[b, 0:CONV_STATE - 1, :] = stb[1:CONV_STATE, :]
        nc_ref[b, CONV_STATE - 1:CONV_STATE, :] = glur

        s = lax.dot_general(qp.astype(jnp.bfloat16), ckb.astype(jnp.bfloat16), (((1,), (1,)), ((), ())),
                            preferred_element_type=jnp.float32) - bias
        s_new = jnp.sum(qp * knr, axis=-1, keepdims=True)
        m = jnp.maximum(jnp.maximum(jnp.max(s, axis=-1, keepdims=True), s_new), sink)
        ex = jnp.exp(s - m)
        ex_new = jnp.exp(s_new - m)
        den = jnp.sum(ex, axis=-1, keepdims=True) + ex_new + jnp.exp(sink - m)
        o = jnp.dot((ex / den).astype(jnp.bfloat16), cvb.astype(jnp.bfloat16), preferred_element_type=jnp.float32)
        o = o + (ex_new / den) * vnr
        att = [jnp.where(row_lo, o[2 * c:2 * c + 1, :], o[2 * c + 1:2 * c + 2, :]) for c in range(D_ATTN // LANES)]
        conv = jnp.sum(stb * dw_hist, axis=0, keepdims=True) + glur * dw_last
        return jnp.concatenate(att + [conv], axis=1)

    def per_group(gi, carry):
        base = pl.multiple_of(gi * SUBLANES, SUBLANES)
        mixbuf[pl.ds(base, SUBLANES), :] = jnp.concatenate([one_seq(base + u) for u in range(SUBLANES)], axis=0)
        return carry

    lax.fori_loop(0, nb // SUBLANES, per_group, 0)

    cvo = _layernorm_silu(mixbuf[:, D_ATTN:] + dwb_ref[...], lng_ref[...], lnb_ref[...])
    mixbuf[:, D_ATTN:] = cvo
    x1_ref[...] = x + jnp.dot(mixbuf[...].astype(jnp.bfloat16), wout_ref[...], preferred_element_type=jnp.float32)


def _mixer_sample(x, ck, cv, st, sinks, nmix, win, qg, kg, dww, dwb, lng, lnb, wout, nb):
    DB, D = x.shape
    d_in = win.shape[1]
    d_conv = dww.shape[1]
    d_mix = wout.shape[0]
    full = lambda shape: pl.BlockSpec(shape, lambda i: (0,) * len(shape))
    return pl.pallas_call(
        _mixer_sample_kernel,
        grid=(DB // nb,),
        in_specs=[
            pl.BlockSpec(memory_space=pltpu.SMEM),
            pl.BlockSpec((nb, D), lambda i: (i, 0)),
            pl.BlockSpec((nb, WINDOW, D_KV), lambda i: (i, 0, 0)),
            pl.BlockSpec((nb, WINDOW, D_KV), lambda i: (i, 0, 0)),
            pl.BlockSpec((nb, CONV_STATE, d_conv), lambda i: (i, 0, 0)),
            full((1, D)), full((D, d_in)), full((1, D_ATTN)), full((1, D_KV)),
            full((CONV_WIDTH, d_conv)), full((1, d_conv)), full((1, d_conv)), full((1, d_conv)),
            full((d_mix, D)),
        ],
        out_specs=[
            pl.BlockSpec((nb, D), lambda i: (i, 0)),
            pl.BlockSpec((nb, WINDOW, D_KV), lambda i: (i, 0, 0)),
            pl.BlockSpec((nb, WINDOW, D_KV), lambda i: (i, 0, 0)),
            pl.BlockSpec((nb, CONV_STATE, d_conv), lambda i: (i, 0, 0)),
        ],
        out_shape=[
            jax.ShapeDtypeStruct((DB, D), jnp.float32),
            jax.ShapeDtypeStruct((DB, WINDOW, D_KV), jnp.float32),
            jax.ShapeDtypeStruct((DB, WINDOW, D_KV), jnp.float32),
            jax.ShapeDtypeStruct((DB, CONV_STATE, d_conv), jnp.float32),
        ],
        scratch_shapes=[
            pltpu.VMEM((nb, D_ATTN), jnp.float32),
            pltpu.VMEM((nb, D_KV), jnp.float32),
            pltpu.VMEM((nb, D_KV), jnp.float32),
            pltpu.VMEM((nb, d_conv), jnp.float32),
            pltpu.VMEM((nb, d_mix), jnp.float32),
        ],
        compiler_params=pltpu.CompilerParams(dimension_semantics=("arbitrary",), vmem_limit_bytes=VMEM_LIMIT),
        name="mixer_sample",
    )(sinks, x, ck, cv, st, nmix, win, qg, kg, dww, dwb, lng, lnb, wout)


GROUP = SUBLANES
N_SLOTS = 2 * GROUP
DMA_THREADS = 2
BITREV = [0, 4, 2, 6, 1, 5, 3, 7]
PEER_TILE = 2 * GROUP * PEER_HEADS
ROUTE_SLICES = 2 * GROUP


def _sublane_sums(ps, sub):
    def merge(a, b, keep_a, shift):
        x = jnp.where(keep_a, a, pltpu.roll(b, shift, 0))
        y = jnp.where(keep_a, pltpu.roll(a, SUBLANES - shift, 0), b)
        return x + y

    z = [merge(ps[2 * i], ps[2 * i + 1], sub < 4, 4) for i in range(4)]
    w = [merge(z[2 * i], z[2 * i + 1], (sub % 4) < 2, 2) for i in range(2)]
    return merge(w[0], w[1], (sub % 2) < 1, 1)


def _peer_kernel(x1c_ref, x1ahead_ref, x1first_ref, x1second_ref, nffnc_ref, nffn_ref, wqT_ref, sk_ref, cpos_ref,
                 slot_like_ref, tab_ref, out_ref, buf, h2buf, h2n, sa_s, sb_s, va_s, vb_s, ia_s, ib_s, cand_s, cid_s,
                 tops_s, eids_s, eid_hk, ids_v, ids_s, gates, sem, ids_sem):
    step = pl.program_id(0)
    nsteps = pl.num_programs(0)
    T = x1c_ref.shape[0]
    rows = PEER_HEADS * PEER_TOPK
    n_chunks = x1c_ref.shape[1]
    d = n_chunks * x1c_ref.shape[2]
    cur = step % 3
    nxt = (step + 1) % 3
    ahead = (step + 2) % 3

    key_id = lax.broadcasted_iota(jnp.int32, (N_KEYS, LANES), 0)
    neg_pad = jnp.full((N_CAND_PAD - N_CAND, LANES), -jnp.inf, jnp.float32)
    zero_pad = jnp.zeros((N_CAND_PAD - N_CAND, LANES), jnp.int32)

    def route_normalize(x_ref):
        h2n[...] = _rms(x_ref[...], nffn_ref[...]).astype(h2n.dtype)

    def key_scores(hh):
        wq = wqT_ref[pl.ds(pl.multiple_of(hh * 2 * D_HALF, 2 * D_HALF), 2 * D_HALF), :]
        qb = lax.dot_general(wq, h2n[...], (((1,), (1,)), ((), ())),
                             preferred_element_type=jnp.float32).astype(jnp.bfloat16)
        sa_s[...] = jnp.dot(sk_ref[0], qb[:D_HALF], preferred_element_type=jnp.float32)
        sb_s[...] = jnp.dot(sk_ref[1], qb[D_HALF:], preferred_element_type=jnp.float32)

    def key_top_steps(it0, n):
        halves = [(sa_s, va_s, ia_s), (sb_s, vb_s, ib_s)]
        vals = [ref[...] for ref, _, _ in halves]
        for it in range(it0, it0 + n):
            for k, (_, v_ref, i_ref) in enumerate(halves):
                s = vals[k]
                m = jnp.max(s, axis=0, keepdims=True)
                im = jnp.min(jnp.where(s == m, key_id, N_KEYS), axis=0, keepdims=True)
                v_ref[it:it + 1, :] = m
                i_ref[it:it + 1, :] = im
                vals[k] = jnp.where(key_id == im, -jnp.inf, s)
        for k, (ref, _, _) in enumerate(halves):
            ref[...] = vals[k]

    def cand_build():
        va, vb, ia, ib = va_s[...], vb_s[...], ia_s[...], ib_s[...]
        cand, cid = [], []
        for a in range(PEER_TOPK):
            nbb = PEER_TOPK // (a + 1)
            cand.append(va[a:a + 1] + vb[0:nbb])
            cid.append(ia[a:a + 1] * N_KEYS + ib[0:nbb])
        cand_s[...] = jnp.concatenate(cand + [neg_pad], axis=0)
        cid_s[...] = jnp.concatenate(cid + [zero_pad], axis=0)

    def cand_top_steps(it0, n):
        cand = cand_s[...]
        cid = cid_s[...]
        cpos = cpos_ref[...]
        for it in range(it0, it0 + n):
            m = jnp.max(cand, axis=0, keepdims=True)
            pm = jnp.min(jnp.where(cand == m, cpos, PEER_TOPK * PEER_TOPK), axis=0, keepdims=True)
            hit = cpos == pm
            tops_s[it:it + 1, :] = m
            eids_s[it:it + 1, :] = jnp.max(jnp.where(hit, cid, -1), axis=0, keepdims=True)
            cand = jnp.where(hit, -jnp.inf, cand)
        cand_s[...] = cand

    def head_finish(hh, dst):
        ts = tops_s[...]
        ex = jnp.exp(ts - ts[0:1])
        orow = pl.ds(pl.multiple_of(hh * PEER_TOPK, PEER_TOPK), PEER_TOPK)
        eid_hk[orow, :] = eids_s[...]
        gates[dst, orow, :] = ex / jnp.sum(ex, axis=0, keepdims=True)

    per_slice = 2 * PEER_TOPK // ROUTE_SLICES

    def route_slice(hh, r, dst):
        half = ROUTE_SLICES // 2
        if r == 0:
            key_scores(hh)
        if r < half:
            key_top_steps(r * per_slice, per_slice)
        if r == half:
            cand_build()
        if r >= half:
            cand_top_steps((r - half) * per_slice, per_slice)
        if r == ROUTE_SLICES - 1:
            head_finish(hh, dst)

    def ids_copy(dst):
        return pltpu.make_async_copy(ids_v, ids_s.at[dst], ids_sem)

    def route_publish(dst):
        ids_v[...] = eid_hk[...].T
        ids_copy(dst).start()

    def issue_token(src, tok, slot):
        for j in range(rows):
            pltpu.make_async_copy(tab_ref.at[ids_s[src, tok, j]], buf.at[slot, j // SUBLANES, :, j % SUBLANES],
                                  sem.at[slot]).start(priority=j % DMA_THREADS)

    def wait_token(slot):
        pltpu.make_async_copy(slot_like_ref, buf.at[slot], sem.at[slot]).wait()

    lane_id = lax.broadcasted_iota(jnp.int32, (rows, T), 1)
    sub = lax.broadcasted_iota(jnp.int32, (SUBLANES, LANES), 0)
    inv_sqrt2 = 1.0 / math.sqrt(2.0)

    def compute_token(t, slot):
        h = h2buf[t]
        gcol = jnp.sum(jnp.where(lane_id == t, gates[cur], 0.0), axis=1, keepdims=True)
        hq = [jnp.broadcast_to(h[q:q + 1, :], (SUBLANES, LANES)) for q in range(n_chunks)]
        acc = [jnp.zeros((SUBLANES, LANES), jnp.float32) for _ in range(n_chunks)]
        for jg in range(rows // SUBLANES):
            r0 = jg * SUBLANES
            prod = [buf[slot, jg, q] * hq[q] for q in range(n_chunks)]
            while len(prod) > 1:
                prod = [prod[k] + prod[k + 1] for k in range(0, len(prod), 2)]
            sc = jnp.sum(prod[0], axis=1, keepdims=True)
            act = 0.5 * sc * (1.0 + lax.erf(sc * inv_sqrt2))
            wb = jnp.broadcast_to(gcol[r0:r0 + SUBLANES] * act, (SUBLANES, LANES))
            for q in range(n_chunks):
                acc[q] = acc[q] + buf[slot, jg, n_chunks + q] * wb
        out_ref[t] = _sublane_sums([acc[BITREV[k]] for k in range(SUBLANES)], sub)

    def run_group(first_tok, par, issue_next, head):
        for u in range(GROUP):
            if issue_next is not None:
                issue_next(u, (1 - par) * GROUP + u)
            slot = par * GROUP + u
            wait_token(slot)
            compute_token(first_tok + u, slot)
            route_slice(head, par * GROUP + u, ahead)

    @pl.when(step == 0)
    def _():
        for x_ref, dst in ((x1first_ref, 0), (x1second_ref, 1)):
            route_normalize(x_ref)

            def head(hh, carry):
                for r in range(ROUTE_SLICES):
                    route_slice(hh, r, dst)
                return carry

            lax.fori_loop(0, PEER_HEADS, head, 0)
            route_publish(dst)
            ids_copy(dst).wait()
        for u in range(GROUP):
            issue_token(0, u, u)

    x = x1c_ref[...]
    ms = jnp.sum(jnp.sum(x * x, axis=2, keepdims=True), axis=1, keepdims=True) * (1.0 / d)
    h2buf[...] = x * lax.rsqrt(ms + EPS) * nffnc_ref[...]
    route_normalize(x1ahead_ref)

    def first_group(first_tok, head):
        run_group(first_tok, 0, lambda u, slot: issue_token(cur, first_tok + GROUP + u, slot), head)

    def body(pi, carry):
        first_tok = pi * 2 * GROUP
        first_group(first_tok, pi)
        run_group(first_tok + GROUP, 1, lambda u, slot: issue_token(cur, first_tok + 2 * GROUP + u, slot), pi)
        return carry

    lax.fori_loop(0, PEER_HEADS - 1, body, 0)

    last_tok = (PEER_HEADS - 1) * 2 * GROUP
    first_group(last_tok, PEER_HEADS - 1)

    @pl.when(step + 1 < nsteps)
    def _():
        for u in range(GROUP):
            issue_token(nxt, u, u)

    run_group(last_tok + GROUP, 1, None, PEER_HEADS - 1)
    route_publish(ahead)
    ids_copy(ahead).wait()


def _peer(x1, nffn, wqT, sk, cpos, tab):
    N, D = x1.shape
    T = PEER_TILE
    rows = PEER_HEADS * PEER_TOPK
    chunks = D // LANES
    nq = wqT.shape[0]
    assert chunks == SUBLANES and N % T == 0 and T == LANES
    nsteps = N // T
    slot_shape = (rows // SUBLANES, 2 * chunks, SUBLANES, LANES)
    slot_like = jnp.zeros(slot_shape, jnp.float32)
    full = lambda shape: pl.BlockSpec(shape, lambda i: (0,) * len(shape))
    out = pl.pallas_call(
        _peer_kernel,
        grid=(nsteps,),
        in_specs=[
            pl.BlockSpec((T, chunks, LANES), lambda i: (i, 0, 0)),
            pl.BlockSpec((T, D), lambda i: (jnp.minimum(i + 2, nsteps - 1), 0)),
            pl.BlockSpec((T, D), lambda i: (0, 0)),
            pl.BlockSpec((T, D), lambda i: (min(1, nsteps - 1), 0)),
            full((chunks, LANES)), full((1, D)), full((nq, D)), full((2, N_KEYS, D_HALF)), full((N_CAND_PAD, LANES)),
            pl.BlockSpec(memory_space=pl.ANY),
            pl.BlockSpec(memory_space=pl.ANY),
        ],
        out_specs=pl.BlockSpec((T, chunks, LANES), lambda i: (i, 0, 0)),
        out_shape=jax.ShapeDtypeStruct((N, chunks, LANES), jnp.float32),
        scratch_shapes=[
            pltpu.VMEM((N_SLOTS,) + slot_shape, jnp.float32),
            pltpu.VMEM((T, chunks, LANES), jnp.float32),
            pltpu.VMEM((T, D), jnp.bfloat16),
            pltpu.VMEM((N_KEYS, T), jnp.float32), pltpu.VMEM((N_KEYS, T), jnp.float32),
            pltpu.VMEM((PEER_TOPK, T), jnp.float32), pltpu.VMEM((PEER_TOPK, T), jnp.float32),
            pltpu.VMEM((PEER_TOPK, T), jnp.int32), pltpu.VMEM((PEER_TOPK, T), jnp.int32),
            pltpu.VMEM((N_CAND_PAD, T), jnp.float32), pltpu.VMEM((N_CAND_PAD, T), jnp.int32),
            pltpu.VMEM((PEER_TOPK, T), jnp.float32), pltpu.VMEM((PEER_TOPK, T), jnp.int32),
            pltpu.VMEM((rows, T), jnp.int32),
            pltpu.VMEM((T, rows), jnp.int32),
            pltpu.SMEM((3, T, rows), jnp.int32),
            pltpu.VMEM((3, rows, T), jnp.float32),
            pltpu.SemaphoreType.DMA((N_SLOTS,)),
            pltpu.SemaphoreType.DMA,
        ],
        compiler_params=pltpu.CompilerParams(dimension_semantics=("arbitrary",), vmem_limit_bytes=VMEM_LIMIT),
        name="peer",
    )(x1.reshape(N, chunks, LANES), x1, x1, x1, nffn.reshape(chunks, LANES), nffn, wqT, sk, cpos, slot_like, tab)
    return out.reshape(N, D)


def _finish_kernel(x1_ref, peer_ref, p_ref, nple_ref, wgate_ref, wproj_ref, y_ref):
    x2 = x1_ref[...] + peer_ref[...]
    hp = _rms(x2, nple_ref[...])
    gate = jax.nn.sigmoid(jnp.dot(hp.astype(jnp.bfloat16), wgate_ref[...], preferred_element_type=jnp.float32))
    proj = jnp.dot(p_ref[...].astype(jnp.bfloat16), wproj_ref[...], preferred_element_type=jnp.float32)
    y_ref[...] = x2 + proj * gate


def _finish(x1, peer, p, nple, wgate, wproj, tile):
    N, D = x1.shape
    dp = p.shape[1]
    full = lambda shape: pl.BlockSpec(shape, lambda i: (0,) * len(shape))
    return pl.pallas_call(
        _finish_kernel,
        grid=(N // tile,),
        in_specs=[
            pl.BlockSpec((tile, D), lambda i: (i, 0)),
            pl.BlockSpec((tile, D), lambda i: (i, 0)),
            pl.BlockSpec((tile, dp), lambda i: (i, 0)),
            full((1, D)), full((D, D)), full((dp, D)),
        ],
        out_specs=pl.BlockSpec((tile, D), lambda i: (i, 0)),
        out_shape=jax.ShapeDtypeStruct((N, D), jnp.float32),
        compiler_params=pltpu.CompilerParams(dimension_semantics=("arbitrary",), vmem_limit_bytes=VMEM_LIMIT),
        name="finish",
    )(x1, peer, p, nple, wgate, wproj)


def _tile_for(n, pref):
    t = min(pref, n)
    while n % t:
        t //= 2
    return t


def kernel(x_prompt, x_sample, cache_k, cache_v, state_conv, p_prompt, p_sample, norm_mix, w_in, q_gain, k_gain,
           attn_sinks, conv_dw_w, conv_dw_b, conv_ln_g, conv_ln_b, w_out, norm_ffn, peer_wq, peer_sub_keys, peer_u,
           peer_v, norm_ple, w_ple_proj, w_ple_gate):
    depth = norm_mix.shape[0]
    assert depth == 1
    B, S, D = x_prompt.shape
    DB, dec_seq, _ = x_sample.shape
    assert dec_seq == 1
    bf = jnp.bfloat16
    i = 0

    qperm = np.concatenate([np.arange(h * HEAD_DIM, (h + 1) * HEAD_DIM) for h in HEAD_ORDER])
    win = jnp.concatenate([w_in[i][:, qperm], w_in[i][:, D_ATTN:]], axis=1).astype(bf)
    wout = jnp.concatenate([w_out[i][qperm], w_out[i][D_ATTN:]], axis=0).astype(bf)
    qg = jnp.tile(q_gain[i], N_Q_HEADS)[None]
    kg = jnp.tile(k_gain[i], N_KV_HEADS)[None]
    sinks = attn_sinks[i]
    nmix = norm_mix[i][None]
    dww, dwb, lng, lnb = conv_dw_w[i], conv_dw_b[i][None], conv_ln_g[i][None], conv_ln_b[i][None]
    nffn = norm_ffn[i][None]
    wqT = peer_wq[i].T.astype(bf)
    sk = peer_sub_keys[i].astype(bf)
    n_exp = peer_u.shape[1]
    tab = jnp.concatenate([peer_u[i].reshape(n_exp, D // LANES, LANES), peer_v[i].reshape(n_exp, D // LANES, LANES)],
                          axis=1)
    nple = norm_ple[i][None]
    wgate = w_ple_gate[i].astype(bf)
    wproj = w_ple_proj[i].astype(bf)
    cpos = np.full((N_CAND_PAD, LANES), PEER_TOPK * PEER_TOPK, np.int32)
    for r, (a, b) in enumerate(CAND_AB):
        cpos[r, :] = a * PEER_TOPK + b
    cpos = jnp.asarray(cpos)

    def tail(x1, p):
        n = x1.shape[0]
        peer = _peer(x1, nffn, wqT, sk, cpos, tab)
        return _finish(x1, peer, p, nple, wgate, wproj, _tile_for(n, 512))

    x1p, nkp, nvp, ncp = _mixer_prompt(x_prompt, sinks, nmix, win, qg, kg, dww, dwb, lng, lnb, wout, _tile_for(S, 512))
    yp = tail(x1p.reshape(B * S, D), p_prompt[i].reshape(B * S, -1)).reshape(B, S, D)

    ck = cache_k[i].reshape(DB, WINDOW, D_KV)
    cv = cache_v[i].reshape(DB, WINDOW, D_KV)
    x1s, nks, nvs, ncs = _mixer_sample(x_sample.reshape(DB, D), ck, cv, state_conv[i], sinks, nmix, win, qg, kg,
                                       dww, dwb, lng, lnb, wout, _tile_for(DB, 32))
    ys = tail(x1s, p_sample[i].reshape(DB, -1)).reshape(DB, 1, D)

    kv5 = lambda a: a.reshape(1, a.shape[0], WINDOW, N_KV_HEADS, HEAD_DIM)
    return (yp, ys, kv5(nkp), kv5(nvp), ncp[None], kv5(nks), kv5(nvs), ncs[None])
```

```python
import functools
import math

import numpy as np
import jax
import jax.numpy as jnp
from jax import lax
from jax.experimental import pallas as pl
from jax.experimental.pallas import tpu as pltpu

HEAD_DIM = 64
N_Q_HEADS = 8
N_KV_HEADS = 2
Q_PER_KV = N_Q_HEADS // N_KV_HEADS
D_ATTN = N_Q_HEADS * HEAD_DIM
D_KV = N_KV_HEADS * HEAD_DIM
WINDOW = 128
ATT_BLOCK = 128
CONV_WIDTH = 31
CONV_STATE = CONV_WIDTH - 1
N_KEYS = 128
PEER_HEADS = 8
PEER_TOPK = 16
D_HALF = 128
EPS = 1e-6
NEG = -1e30
LANES = 128
SUBLANES = 8
CONV_PAD = 32
VMEM_LIMIT = 56 * 1024 * 1024

HEAD_ORDER = [0, 4, 1, 5, 2, 6, 3, 7]
SLOPES = [2.0 ** (-8.0 * (h + 1) / N_Q_HEADS) for h in range(N_Q_HEADS)]

CAND_AB = [(a, b) for a in range(PEER_TOPK) for b in range(PEER_TOPK // (a + 1))]
N_CAND = len(CAND_AB)
N_CAND_PAD = -(-N_CAND // 8) * 8


def _rms(x, g):
    return x * lax.rsqrt(jnp.mean(x * x, axis=-1, keepdims=True) + EPS) * g


def _head_rms(a, gain, lane_lo):
    cols = []
    for c in range(a.shape[1] // LANES):
        ac = a[:, c * LANES:(c + 1) * LANES]
        sq = ac * ac
        lo = jnp.sum(jnp.where(lane_lo, sq, 0.0), axis=-1, keepdims=True)
        hi = jnp.sum(jnp.where(lane_lo, 0.0, sq), axis=-1, keepdims=True)
        ms = jnp.where(lane_lo, lo, hi) * (1.0 / HEAD_DIM)
        cols.append(ac * lax.rsqrt(ms + EPS) * gain[:, c * LANES:(c + 1) * LANES])
    return cols


def _layernorm_silu(y, g, b):
    mu = jnp.mean(y, axis=-1, keepdims=True)
    yc = y - mu
    yn = yc * lax.rsqrt(jnp.mean(yc * yc, axis=-1, keepdims=True) + EPS) * g + b
    return yn * jax.nn.sigmoid(yn)


def _mixer_prompt_kernel(sinks_ref, x_ref, nmix_ref, win_ref, qg_ref, kg_ref, dww_ref, dwb_ref, lng_ref, lnb_ref,
                         wout_ref, x1_ref, nk_ref, nv_ref, nc_ref, kbuf, vbuf, gbuf, mixbuf):
    si = pl.program_id(1)
    T = x_ref.shape[1]
    d_conv = gbuf.shape[1]
    nblk = T // ATT_BLOCK

    @pl.when(si == 0)
    def _():
        kbuf[0:ATT_BLOCK, :] = jnp.zeros((ATT_BLOCK, D_KV), kbuf.dtype)
        vbuf[0:ATT_BLOCK, :] = jnp.zeros((ATT_BLOCK, D_KV), vbuf.dtype)
        gbuf[0:CONV_PAD, :] = jnp.zeros((CONV_PAD, d_conv), gbuf.dtype)

    x = x_ref[0]
    h = _rms(x, nmix_ref[...])
    z = jnp.dot(h.astype(jnp.bfloat16), win_ref[...], preferred_element_type=jnp.float32)
    o1, o2, o3, o4 = D_ATTN, D_ATTN + D_KV, D_ATTN + 2 * D_KV, D_ATTN + 2 * D_KV + d_conv
    lane_lo = lax.broadcasted_iota(jnp.int32, (T, LANES), 1) < HEAD_DIM

    qcols = _head_rms(z[:, :o1], qg_ref[...], lane_lo)
    kn = _head_rms(z[:, o1:o2], kg_ref[...], lane_lo)[0]
    v = z[:, o2:o3]
    glu = z[:, o3:o4] * jax.nn.sigmoid(z[:, o4:])

    kbuf[ATT_BLOCK:ATT_BLOCK + T, :] = kn.astype(kbuf.dtype)
    vbuf[ATT_BLOCK:ATT_BLOCK + T, :] = v.astype(vbuf.dtype)
    gbuf[CONV_PAD:CONV_PAD + T, :] = glu

    @pl.when(si == pl.num_programs(1) - 1)
    def _():
        nk_ref[0] = kn[T - WINDOW:, :]
        nv_ref[0] = v[T - WINDOW:, :]
        nc_ref[0] = glu[T - CONV_STATE:, :]

    rows = 32
    dwb = dwb_ref[...]
    lng = lng_ref[...]
    lnb = lnb_ref[...]
    first = CONV_PAD - CONV_STATE

    def conv_chunk(base):
        acc = jnp.zeros((rows, d_conv), jnp.float32)
        win = gbuf[base:base + rows + CONV_PAD, :]
        for r in range(SUBLANES):
            span = (rows + CONV_PAD - r) // SUBLANES * SUBLANES
            shifted = win[r:r + span, :]
            for w in range(CONV_WIDTH):
                if (first + w) % SUBLANES == r:
                    k = first + w - r
                    acc = acc + shifted[k:k + rows, :] * dww_ref[w:w + 1, :]
        cv = _layernorm_silu(acc + dwb, lng, lnb)
        mixbuf[base:base + rows, D_ATTN:] = cv.astype(mixbuf.dtype)

    ii = lax.broadcasted_iota(jnp.int32, (ATT_BLOCK, 2 * ATT_BLOCK), 0)
    jj = lax.broadcasted_iota(jnp.int32, (ATT_BLOCK, 2 * ATT_BLOCK), 1)
    dist = ATT_BLOCK + ii - jj
    band = (dist >= 0) & (dist <= WINDOW)
    distf = dist.astype(jnp.float32)
    lane_lo_b = lax.broadcasted_iota(jnp.int32, (ATT_BLOCK, LANES), 1) < HEAD_DIM
    scale = HEAD_DIM ** -0.5
    for blk in range(nblk):
        kk = kbuf[blk * ATT_BLOCK:(blk + 2) * ATT_BLOCK, :]
        vv = vbuf[blk * ATT_BLOCK:(blk + 2) * ATT_BLOCK, :]
        if blk == 0:
            valid = band & ((jj >= ATT_BLOCK) | (si > 0))
        else:
            valid = band
        for c in range(D_ATTN // LANES):
            qc = qcols[c][blk * ATT_BLOCK:(blk + 1) * ATT_BLOCK, :] * scale
            outs = []
            for e in range(2):
                head = HEAD_ORDER[2 * c + e]
                qe = jnp.where(lane_lo_b if e == 0 else ~lane_lo_b, qc, 0.0).astype(jnp.bfloat16)
                s = lax.dot_general(qe, kk, (((1,), (1,)), ((), ())), preferred_element_type=jnp.float32)
                s = s - SLOPES[head] * distf
                s = jnp.where(valid, s, NEG)
                sink = sinks_ref[head]
                m = jnp.maximum(jnp.max(s, axis=-1, keepdims=True), sink)
                ex = jnp.exp(s - m)
                den = jnp.sum(ex, axis=-1, keepdims=True) + jnp.exp(sink - m)
                p = ex / den
                outs.append(jnp.dot(p.astype(jnp.bfloat16), vv, preferred_element_type=jnp.float32))
            att = jnp.where(lane_lo_b, outs[0], outs[1])
            mixbuf[blk * ATT_BLOCK:(blk + 1) * ATT_BLOCK, c * LANES:(c + 1) * LANES] = att.astype(mixbuf.dtype)
        for cc in range(ATT_BLOCK // rows):
            conv_chunk(blk * ATT_BLOCK + cc * rows)

    x1_ref[0] = x + jnp.dot(mixbuf[...], wout_ref[...], preferred_element_type=jnp.float32)

    kbuf[0:ATT_BLOCK, :] = kbuf[T:T + ATT_BLOCK, :]
    vbuf[0:ATT_BLOCK, :] = vbuf[T:T + ATT_BLOCK, :]
    gbuf[0:CONV_PAD, :] = gbuf[T:T + CONV_PAD, :]


def _mixer_prompt(x, sinks, nmix, win, qg, kg, dww, dwb, lng, lnb, wout, tile):
    B, S, D = x.shape
    d_in = win.shape[1]
    d_conv = dww.shape[1]
    d_mix = wout.shape[0]
    full = lambda shape: pl.BlockSpec(shape, lambda b, s: (0,) * len(shape))
    return pl.pallas_call(
        _mixer_prompt_kernel,
        grid=(B, S // tile),
        in_specs=[
            pl.BlockSpec(memory_space=pltpu.SMEM),
            pl.BlockSpec((1, tile, D), lambda b, s: (b, s, 0)),
            full((1, D)), full((D, d_in)), full((1, D_ATTN)), full((1, D_KV)),
            full((CONV_WIDTH, d_conv)), full((1, d_conv)), full((1, d_conv)), full((1, d_conv)),
            full((d_mix, D)),
        ],
        out_specs=[
            pl.BlockSpec((1, tile, D), lambda b, s: (b, s, 0)),
            pl.BlockSpec((1, WINDOW, D_KV), lambda b, s: (b, 0, 0)),
            pl.BlockSpec((1, WINDOW, D_KV), lambda b, s: (b, 0, 0)),
            pl.BlockSpec((1, CONV_STATE, d_conv), lambda b, s: (b, 0, 0)),
        ],
        out_shape=[
            jax.ShapeDtypeStruct((B, S, D), jnp.float32),
            jax.ShapeDtypeStruct((B, WINDOW, D_KV), jnp.float32),
            jax.ShapeDtypeStruct((B, WINDOW, D_KV), jnp.float32),
            jax.ShapeDtypeStruct((B, CONV_STATE, d_conv), jnp.float32),
        ],
        scratch_shapes=[
            pltpu.VMEM((tile + ATT_BLOCK, D_KV), jnp.bfloat16),
            pltpu.VMEM((tile + ATT_BLOCK, D_KV), jnp.bfloat16),
            pltpu.VMEM((tile + CONV_PAD, d_conv), jnp.float32),
            pltpu.VMEM((tile, d_mix), jnp.bfloat16),
        ],
        compiler_params=pltpu.CompilerParams(
            dimension_semantics=("arbitrary", "arbitrary"), vmem_limit_bytes=VMEM_LIMIT),
        name="mixer_prompt",
    )(sinks, x, nmix, win, qg, kg, dww, dwb, lng, lnb, wout)


def _mixer_sample_kernel(sinks_ref, x_ref, ck_ref, cv_ref, st_ref, nmix_ref, win_ref, qg_ref, kg_ref, dww_ref,
                         dwb_ref, lng_ref, lnb_ref, wout_ref, x1_ref, nk_ref, nv_ref, nc_ref,
                         qbuf, knbuf, vnbuf, glubuf, mixbuf):
    nb = x_ref.shape[0]
    d_conv = st_ref.shape[2]
    x = x_ref[...]
    h = _rms(x, nmix_ref[...])
    z = jnp.dot(h.astype(jnp.bfloat16), win_ref[...], preferred_element_type=jnp.float32)
    o1, o2, o3, o4 = D_ATTN, D_ATTN + D_KV, D_ATTN + 2 * D_KV, D_ATTN + 2 * D_KV + d_conv
    lane_lo = lax.broadcasted_iota(jnp.int32, (nb, LANES), 1) < HEAD_DIM
    qcols = _head_rms(z[:, :o1], qg_ref[...], lane_lo)
    kn = _head_rms(z[:, o1:o2], kg_ref[...], lane_lo)[0]
    v = z[:, o2:o3]
    glu = z[:, o3:o4] * jax.nn.sigmoid(z[:, o4:])
    for c in range(D_ATTN // LANES):
        qbuf[:, c * LANES:(c + 1) * LANES] = qcols[c] * (HEAD_DIM ** -0.5)
    knbuf[...] = kn
    vnbuf[...] = v
    glubuf[...] = glu

    row_lo = lax.broadcasted_iota(jnp.int32, (1, LANES), 1) < HEAD_DIM
    hrow = lax.broadcasted_iota(jnp.int32, (N_Q_HEADS, LANES), 0)
    hlane = lax.broadcasted_iota(jnp.int32, (N_Q_HEADS, LANES), 1)
    keep = (hrow % 2) == (hlane // HEAD_DIM)
    jcol = lax.broadcasted_iota(jnp.int32, (N_Q_HEADS, WINDOW), 1)
    distf = (WINDOW - jcol).astype(jnp.float32)
    hcol = lax.broadcasted_iota(jnp.int32, (N_Q_HEADS, 1), 0)
    slope = jnp.zeros((N_Q_HEADS, 1), jnp.float32)
    sink = jnp.zeros((N_Q_HEADS, 1), jnp.float32)
    for r in range(N_Q_HEADS):
        slope = jnp.where(hcol == r, SLOPES[HEAD_ORDER[r]], slope)
        sink = jnp.where(hcol == r, sinks_ref[HEAD_ORDER[r]], sink)
    bias = slope * distf
    dw_hist = dww_ref[0:CONV_STATE, :]
    dw_last = dww_ref[CONV_STATE:CONV_WIDTH, :]

    def one_seq(b):
        qrow = qbuf[pl.ds(b, 1), :]
        qp = jnp.concatenate(
            [jnp.broadcast_to(qrow[:, c * LANES:(c + 1) * LANES], (2, LANES)) for c in range(D_ATTN // LANES)], axis=0)
        qp = jnp.where(keep, qp, 0.0)
        ckb = ck_ref[b]
        cvb = cv_ref[b]
        knr = knbuf[pl.ds(b, 1), :]
        vnr = vnbuf[pl.ds(b, 1), :]
        glur = glubuf[pl.ds(b, 1), :]
        stb = st_ref[b]

        nk_ref[b, 0:WINDOW - 1, :] = ckb[1:WINDOW, :]
        nk_ref[b, WINDOW - 1:WINDOW, :] = knr
        nv_ref[b, 0:WINDOW - 1, :] = cvb[1:WINDOW, :]
        nv_ref[b, WINDOW - 1:WINDOW, :] = vnr
        nc_ref[b, 0:CONV_STATE - 1, :] = stb[1:CONV_STATE, :]
        nc_ref[b, CONV_STATE - 1:CONV_STATE, :] = glur

        s = lax.dot_general(qp.astype(jnp.bfloat16), ckb.astype(jnp.bfloat16), (((1,), (1,)), ((), ())),
                            preferred_element_type=jnp.float32) - bias
        s_new = jnp.sum(qp * knr, axis=-1, keepdims=True)
        m = jnp.maximum(jnp.maximum(jnp.max(s, axis=-1, keepdims=True), s_new), sink)
        ex = jnp.exp(s - m)
        ex_new = jnp.exp(s_new - m)
        den = jnp.sum(ex, axis=-1, keepdims=True) + ex_new + jnp.exp(sink - m)
        o = jnp.dot((ex / den).astype(jnp.bfloat16), cvb.astype(jnp.bfloat16), preferred_element_type=jnp.float32)
        o = o + (ex_new / den) * vnr
        att = [jnp.where(row_lo, o[2 * c:2 * c + 1, :], o[2 * c + 1:2 * c + 2, :]) for c in range(D_ATTN // LANES)]
        conv = jnp.sum(stb * dw_hist, axis=0, keepdims=True) + glur * dw_last
        return jnp.concatenate(att + [conv], axis=1)

    def per_group(gi, carry):
        base = pl.multiple_of(gi * SUBLANES, SUBLANES)
        mixbuf[pl.ds(base, SUBLANES), :] = jnp.concatenate([one_seq(base + u) for u in range(SUBLANES)], axis=0)
        return carry

    lax.fori_loop(0, nb // SUBLANES, per_group, 0)

    cvo = _layernorm_silu(mixbuf[:, D_ATTN:] + dwb_ref[...], lng_ref[...], lnb_ref[...])
    mixbuf[:, D_ATTN:] = cvo
    x1_ref[...] = x + jnp.dot(mixbuf[...].astype(jnp.bfloat16), wout_ref[...], preferred_element_type=jnp.float32)


def _mixer_sample(x, ck, cv, st, sinks, nmix, win, qg, kg, dww, dwb, lng, lnb, wout, nb):
    DB, D = x.shape
    d_in = win.shape[1]
    d_conv = dww.shape[1]
    d_mix = wout.shape[0]
    full = lambda shape: pl.BlockSpec(shape, lambda i: (0,) * len(shape))
    return pl.pallas_call(
        _mixer_sample_kernel,
        grid=(DB // nb,),
        in_specs=[
            pl.BlockSpec(memory_space=pltpu.SMEM),
            pl.BlockSpec((nb, D), lambda i: (i, 0)),
            pl.BlockSpec((nb, WINDOW, D_KV), lambda i: (i, 0, 0)),
            pl.BlockSpec((nb, WINDOW, D_KV), lambda i: (i, 0, 0)),
            pl.BlockSpec((nb, CONV_STATE, d_conv), lambda i: (i, 0, 0)),
            full((1, D)), full((D, d_in)), full((1, D_ATTN)), full((1, D_KV)),
            full((CONV_WIDTH, d_conv)), full((1, d_conv)), full((1, d_conv)), full((1, d_conv)),
            full((d_mix, D)),
        ],
        out_specs=[
            pl.BlockSpec((nb, D), lambda i: (i, 0)),
            pl.BlockSpec((nb, WINDOW, D_KV), lambda i: (i, 0, 0)),
            pl.BlockSpec((nb, WINDOW, D_KV), lambda i: (i, 0, 0)),
            pl.BlockSpec((nb, CONV_STATE, d_conv), lambda i: (i, 0, 0)),
        ],
        out_shape=[
            jax.ShapeDtypeStruct((DB, D), jnp.float32),
            jax.ShapeDtypeStruct((DB, WINDOW, D_KV), jnp.float32),
            jax.ShapeDtypeStruct((DB, WINDOW, D_KV), jnp.float32),
            jax.ShapeDtypeStruct((DB, CONV_STATE, d_conv), jnp.float32),
        ],
        scratch_shapes=[
            pltpu.VMEM((nb, D_ATTN), jnp.float32),
            pltpu.VMEM((nb, D_KV), jnp.float32),
            pltpu.VMEM((nb, D_KV), jnp.float32),
            pltpu.VMEM((nb, d_conv), jnp.float32),
            pltpu.VMEM((nb, d_mix), jnp.float32),
        ],
        compiler_params=pltpu.CompilerParams(dimension_semantics=("arbitrary",), vmem_limit_bytes=VMEM_LIMIT),
        name="mixer_sample",
    )(sinks, x, ck, cv, st, nmix, win, qg, kg, dww, dwb, lng, lnb, wout)


GROUP = SUBLANES
N_SLOTS = 2 * GROUP
DMA_THREADS = 2
BITREV = [0, 4, 2, 6, 1, 5, 3, 7]
PEER_TILE = 2 * GROUP * PEER_HEADS
ROUTE_SLICES = 2 * GROUP
Q_PIECE_ROWS = 2 * LANES
Q_PIECES = 4


def _sublane_sums(ps, sub):
    def merge(a, b, keep_a, shift):
        x = jnp.where(keep_a, a, pltpu.roll(b, shift, 0))
        y = jnp.where(keep_a, pltpu.roll(a, SUBLANES - shift, 0), b)
        return x + y

    z = [merge(ps[2 * i], ps[2 * i + 1], sub < 4, 4) for i in range(4)]
    w = [merge(z[2 * i], z[2 * i + 1], (sub % 4) < 2, 2) for i in range(2)]
    return merge(w[0], w[1], (sub % 2) < 1, 1)


def _peer_kernel(x1c_ref, x1ahead_ref, x1first_ref, x1second_ref, nffn_ref, wq_ref, sk_ref, cpos_ref,
                 slot_like_ref, tab_ref, out_ref, buf, h2buf, h2n, qn, qacc, sa_s, sb_s, va_s, vb_s, ia_s, ib_s, cand_s, cid_s,
                 tops_s, eids_s, eid_hk, ids_v, ids_s, gates, sem, ids_sem):
    step = pl.program_id(0)
    nsteps = pl.num_programs(0)
    T = x1c_ref.shape[0]
    rows = PEER_HEADS * PEER_TOPK
    n_chunks = h2buf.shape[1]
    cur = step % 3
    nxt = (step + 1) % 3
    ahead = (step + 2) % 3

    key_id = lax.broadcasted_iota(jnp.int32, (N_KEYS, LANES), 0)
    neg_pad = jnp.full((N_CAND_PAD - N_CAND, LANES), -jnp.inf, jnp.float32)
    zero_pad = jnp.zeros((N_CAND_PAD - N_CAND, LANES), jnp.int32)

    def route_normalize(x_ref):
        h2n[...] = _rms(x_ref[...], nffn_ref[...]).astype(h2n.dtype)

    def head_cols(hh):
        return pl.ds(pl.multiple_of(hh * 2 * D_HALF, 2 * D_HALF), 2 * D_HALF)

    def q_piece(hh, k):
        kr = slice(k * Q_PIECE_ROWS, (k + 1) * Q_PIECE_ROWS)
        part = jnp.dot(h2n[:, kr], wq_ref[kr, head_cols(hh)], preferred_element_type=jnp.float32)
        if k == 0:
            qacc[...] = part
        elif k < Q_PIECES - 1:
            qacc[...] = qacc[...] + part
        else:
            qn[:, head_cols(hh)] = (qacc[...] + part).astype(qn.dtype)

    def key_scores(hh):
        qb = qn[:, head_cols(hh)]
        nt = (((1,), (1,)), ((), ()))
        sa_s[...] = lax.dot_general(sk_ref[0], qb[:, :D_HALF], nt, preferred_element_type=jnp.float32)
        sb_s[...] = lax.dot_general(sk_ref[1], qb[:, D_HALF:], nt, preferred_element_type=jnp.float32)

    def key_top_steps(it0, n):
        halves = [(sa_s, va_s, ia_s), (sb_s, vb_s, ib_s)]
        vals = [ref[...] for ref, _, _ in halves]
        for it in range(it0, it0 + n):
            for k, (_, v_ref, i_ref) in enumerate(halves):
                s = vals[k]
                m = jnp.max(s, axis=0, keepdims=True)
                im = jnp.min(jnp.where(s == m, key_id, N_KEYS), axis=0, keepdims=True)
                v_ref[it:it + 1, :] = m
                i_ref[it:it + 1, :] = im
                vals[k] = jnp.where(key_id == im, -jnp.inf, s)
        for k, (ref, _, _) in enumerate(halves):
            ref[...] = vals[k]

    def cand_build():
        va, vb, ia, ib = va_s[...], vb_s[...], ia_s[...], ib_s[...]
        cand, cid = [], []
        for a in range(PEER_TOPK):
            nbb = PEER_TOPK // (a + 1)
            cand.append(va[a:a + 1] + vb[0:nbb])
            cid.append(ia[a:a + 1] * N_KEYS + ib[0:nbb])
        cand_s[...] = jnp.concatenate(cand + [neg_pad], axis=0)
        cid_s[...] = jnp.concatenate(cid + [zero_pad], axis=0)

    def cand_top_steps(it0, n):
        cand = cand_s[...]
        cid = cid_s[...]
        cpos = cpos_ref[...]
        for it in range(it0, it0 + n):
            m = jnp.max(cand, axis=0, keepdims=True)
            pm = jnp.min(jnp.where(cand == m, cpos, PEER_TOPK * PEER_TOPK), axis=0, keepdims=True)
            hit = cpos == pm
            tops_s[it:it + 1, :] = m
            eids_s[it:it + 1, :] = jnp.max(jnp.where(hit, cid, -1), axis=0, keepdims=True)
            cand = jnp.where(hit, -jnp.inf, cand)
        cand_s[...] = cand

    def head_finish(hh, dst):
        ts = tops_s[...]
        ex = jnp.exp(ts - ts[0:1])
        orow = pl.ds(pl.multiple_of(hh * PEER_TOPK, PEER_TOPK), PEER_TOPK)
        eid_hk[orow, :] = eids_s[...]
        gates[dst, orow, :] = ex / jnp.sum(ex, axis=0, keepdims=True)

    per_slice = 2 * PEER_TOPK // ROUTE_SLICES

    def route_slice(hh, r, dst, project_next):
        half = ROUTE_SLICES // 2
        if project_next and r % (ROUTE_SLICES // Q_PIECES) == 0:
            q_piece(hh + 1, r // (ROUTE_SLICES // Q_PIECES))
        if r == 0:
            key_scores(hh)
        if r < half:
            key_top_steps(r * per_slice, per_slice)
        if r == half:
            cand_build()
        if r >= half:
            cand_top_steps((r - half) * per_slice, per_slice)
        if r == ROUTE_SLICES - 1:
            head_finish(hh, dst)

    def ids_copy(dst):
        return pltpu.make_async_copy(ids_v, ids_s.at[dst], ids_sem)

    def route_publish(dst):
        ids_v[...] = eid_hk[...].T
        ids_copy(dst).start()

    def issue_token(src, tok, slot):
        for j in range(rows):
            pltpu.make_async_copy(tab_ref.at[ids_s[src, tok, j]], buf.at[slot, j // SUBLANES, :, j % SUBLANES],
                                  sem.at[slot]).start(priority=j % DMA_THREADS)

    def wait_token(slot):
        pltpu.make_async_copy(slot_like_ref, buf.at[slot], sem.at[slot]).wait()

    lane_id = lax.broadcasted_iota(jnp.int32, (rows, T), 1)
    sub = lax.broadcasted_iota(jnp.int32, (SUBLANES, LANES), 0)
    inv_sqrt2 = 1.0 / math.sqrt(2.0)

    def compute_token(t, slot):
        h = h2buf[t]
        gcol = jnp.sum(jnp.where(lane_id == t, gates[cur], 0.0), axis=1, keepdims=True)
        hq = [jnp.broadcast_to(h[q:q + 1, :], (SUBLANES, LANES)) for q in range(n_chunks)]
        acc = [jnp.zeros((SUBLANES, LANES), jnp.float32) for _ in range(n_chunks)]
        for jg in range(rows // SUBLANES):
            r0 = jg * SUBLANES
            prod = [buf[slot, jg, q] * hq[q] for q in range(n_chunks)]
            while len(prod) > 1:
                prod = [prod[k] + prod[k + 1] for k in range(0, len(prod), 2)]
            sc = jnp.sum(prod[0], axis=1, keepdims=True)
            act = 0.5 * sc * (1.0 + lax.erf(sc * inv_sqrt2))
            wb = jnp.broadcast_to(gcol[r0:r0 + SUBLANES] * act, (SUBLANES, LANES))
            for q in range(n_chunks):
                acc[q] = acc[q] + buf[slot, jg, n_chunks + q] * wb
        out_ref[t] = _sublane_sums([acc[BITREV[k]] for k in range(SUBLANES)], sub)

    def run_group(first_tok, par, issue_next, head, project_next):
        for u in range(GROUP):
            if issue_next is not None:
                issue_next(u, (1 - par) * GROUP + u)
            slot = par * GROUP + u
            wait_token(slot)
            compute_token(first_tok + u, slot)
            route_slice(head, par * GROUP + u, ahead, project_next)

    @pl.when(step == 0)
    def _():
        for x_ref, dst in ((x1first_ref, 0), (x1second_ref, 1)):
            route_normalize(x_ref)

            def head(hh, carry):
                for k in range(Q_PIECES):
                    q_piece(hh, k)
                for r in range(ROUTE_SLICES):
                    route_slice(hh, r, dst, False)
                return carry

            lax.fori_loop(0, PEER_HEADS, head, 0)
            route_publish(dst)
            ids_copy(dst).wait()
        for u in range(GROUP):
            issue_token(0, u, u)

    h2 = _rms(x1c_ref[...], nffn_ref[...])
    for q in range(n_chunks):
        h2buf[:, q, :] = h2[:, q * LANES:(q + 1) * LANES]
    route_normalize(x1ahead_ref)
    for k in range(Q_PIECES):
        q_piece(0, k)

    def first_group(first_tok, head, project_next):
        run_group(first_tok, 0, lambda u, slot: issue_token(cur, first_tok + GROUP + u, slot), head, project_next)

    def body(pi, carry):
        first_tok = pi * 2 * GROUP
        first_group(first_tok, pi, True)
        run_group(first_tok + GROUP, 1, lambda u, slot: issue_token(cur, first_tok + 2 * GROUP + u, slot), pi, True)
        return carry

    lax.fori_loop(0, PEER_HEADS - 1, body, 0)

    last_tok = (PEER_HEADS - 1) * 2 * GROUP
    first_group(last_tok, PEER_HEADS - 1, False)

    @pl.when(step + 1 < nsteps)
    def _():
        for u in range(GROUP):
            issue_token(nxt, u, u)

    run_group(last_tok + GROUP, 1, None, PEER_HEADS - 1, False)
    route_publish(ahead)
    ids_copy(ahead).wait()


def _peer(x1, nffn, wq, sk, cpos, tab):
    N, D = x1.shape
    T = PEER_TILE
    rows = PEER_HEADS * PEER_TOPK
    chunks = D // LANES
    nq = wq.shape[1]
    assert chunks == SUBLANES and N % T == 0 and T == LANES
    nsteps = N // T
    slot_shape = (rows // SUBLANES, 2 * chunks, SUBLANES, LANES)
    slot_like = jnp.zeros(slot_shape, jnp.float32)
    full = lambda shape: pl.BlockSpec(shape, lambda i: (0,) * len(shape))
    out = pl.pallas_call(
        _peer_kernel,
        grid=(nsteps,),
        in_specs=[
            pl.BlockSpec((T, D), lambda i: (i, 0)),
            pl.BlockSpec((T, D), lambda i: (jnp.minimum(i + 2, nsteps - 1), 0)),
            pl.BlockSpec((T, D), lambda i: (0, 0)),
            pl.BlockSpec((T, D), lambda i: (min(1, nsteps - 1), 0)),
            full((1, D)), full((D, nq)), full((2, N_KEYS, D_HALF)), full((N_CAND_PAD, LANES)),
            pl.BlockSpec(memory_space=pl.ANY),
            pl.BlockSpec(memory_space=pl.ANY),
        ],
        out_specs=pl.BlockSpec((T, chunks, LANES), lambda i: (i, 0, 0)),
        out_shape=jax.ShapeDtypeStruct((N, chunks, LANES), jnp.float32),
        scratch_shapes=[
            pltpu.VMEM((N_SLOTS,) + slot_shape, jnp.float32),
            pltpu.VMEM((T, chunks, LANES), jnp.float32),
            pltpu.VMEM((T, D), jnp.bfloat16), pltpu.VMEM((T, nq), jnp.bfloat16),
            pltpu.VMEM((T, 2 * D_HALF), jnp.float32),
            pltpu.VMEM((N_KEYS, T), jnp.float32), pltpu.VMEM((N_KEYS, T), jnp.float32),
            pltpu.VMEM((PEER_TOPK, T), jnp.float32), pltpu.VMEM((PEER_TOPK, T), jnp.float32),
            pltpu.VMEM((PEER_TOPK, T), jnp.int32), pltpu.VMEM((PEER_TOPK, T), jnp.int32),
            pltpu.VMEM((N_CAND_PAD, T), jnp.float32), pltpu.VMEM((N_CAND_PAD, T), jnp.int32),
            pltpu.VMEM((PEER_TOPK, T), jnp.float32), pltpu.VMEM((PEER_TOPK, T), jnp.int32),
            pltpu.VMEM((rows, T), jnp.int32),
            pltpu.VMEM((T, rows), jnp.int32),
            pltpu.SMEM((3, T, rows), jnp.int32),
            pltpu.VMEM((3, rows, T), jnp.float32),
            pltpu.SemaphoreType.DMA((N_SLOTS,)),
            pltpu.SemaphoreType.DMA,
        ],
        compiler_params=pltpu.CompilerParams(dimension_semantics=("arbitrary",), vmem_limit_bytes=VMEM_LIMIT),
        name="peer",
    )(x1, x1, x1, x1, nffn, wq, sk, cpos, slot_like, tab)
    return out


def _finish_kernel(x1_ref, peer_ref, p_ref, nple_ref, wgate_ref, wproj_ref, y_ref):
    peer = jnp.concatenate([peer_ref[:, q, :] for q in range(peer_ref.shape[1])], axis=1)
    x2 = x1_ref[...] + peer
    hp = _rms(x2, nple_ref[...])
    gate = jax.nn.sigmoid(jnp.dot(hp.astype(jnp.bfloat16), wgate_ref[...], preferred_element_type=jnp.float32))
    proj = jnp.dot(p_ref[...].astype(jnp.bfloat16), wproj_ref[...], preferred_element_type=jnp.float32)
    y_ref[...] = x2 + proj * gate


def _finish(x1, peer, p, nple, wgate, wproj, tile):
    N, D = x1.shape
    dp = p.shape[1]
    full = lambda shape: pl.BlockSpec(shape, lambda i: (0,) * len(shape))
    return pl.pallas_call(
        _finish_kernel,
        grid=(N // tile,),
        in_specs=[
            pl.BlockSpec((tile, D), lambda i: (i, 0)),
            pl.BlockSpec((tile,) + peer.shape[1:], lambda i: (i, 0, 0)),
            pl.BlockSpec((tile, dp), lambda i: (i, 0)),
            full((1, D)), full((D, D)), full((dp, D)),
        ],
        out_specs=pl.BlockSpec((tile, D), lambda i: (i, 0)),
        out_shape=jax.ShapeDtypeStruct((N, D), jnp.float32),
        compiler_params=pltpu.CompilerParams(dimension_semantics=("arbitrary",), vmem_limit_bytes=VMEM_LIMIT),
        name="finish",
    )(x1, peer, p, nple, wgate, wproj)


def _tile_for(n, pref):
    t = min(pref, n)
    while n % t:
        t //= 2
    return t


def kernel(x_prompt, x_sample, cache_k, cache_v, state_conv, p_prompt, p_sample, norm_mix, w_in, q_gain, k_gain,
           attn_sinks, conv_dw_w, conv_dw_b, conv_ln_g, conv_ln_b, w_out, norm_ffn, peer_wq, peer_sub_keys, peer_u,
           peer_v, norm_ple, w_ple_proj, w_ple_gate):
    depth = norm_mix.shape[0]
    assert depth == 1
    B, S, D = x_prompt.shape
    DB, dec_seq, _ = x_sample.shape
    assert dec_seq == 1
    bf = jnp.bfloat16
    i = 0

    qperm = np.concatenate([np.arange(h * HEAD_DIM, (h + 1) * HEAD_DIM) for h in HEAD_ORDER])
    win = jnp.concatenate([w_in[i][:, qperm], w_in[i][:, D_ATTN:]], axis=1).astype(bf)
    wout = jnp.concatenate([w_out[i][qperm], w_out[i][D_ATTN:]], axis=0).astype(bf)
    qg = jnp.tile(q_gain[i], N_Q_HEADS)[None]
    kg = jnp.tile(k_gain[i], N_KV_HEADS)[None]
    sinks = attn_sinks[i]
    nmix = norm_mix[i][None]
    dww, dwb, lng, lnb = conv_dw_w[i], conv_dw_b[i][None], conv_ln_g[i][None], conv_ln_b[i][None]
    nffn = norm_ffn[i][None]
    wq = peer_wq[i].astype(bf)
    sk = peer_sub_keys[i].astype(bf)
    n_exp = peer_u.shape[1]
    tab = jnp.concatenate([peer_u[i].reshape(n_exp, D // LANES, LANES), peer_v[i].reshape(n_exp, D // LANES, LANES)],
                          axis=1)
    nple = norm_ple[i][None]
    wgate = w_ple_gate[i].astype(bf)
    wproj = w_ple_proj[i].astype(bf)
    cpos = np.full((N_CAND_PAD, LANES), PEER_TOPK * PEER_TOPK, np.int32)
    for r, (a, b) in enumerate(CAND_AB):
        cpos[r, :] = a * PEER_TOPK + b
    cpos = jnp.asarray(cpos)

    def tail(x1, p):
        n = x1.shape[0]
        peer = _peer(x1, nffn, wq, sk, cpos, tab)
        return _finish(x1, peer, p, nple, wgate, wproj, _tile_for(n, 512))

    x1p, nkp, nvp, ncp = _mixer_prompt(x_prompt, sinks, nmix, win, qg, kg, dww, dwb, lng, lnb, wout, _tile_for(S, 512))
    yp = tail(x1p.reshape(B * S, D), p_prompt[i].reshape(B * S, -1)).reshape(B, S, D)

    ck = cache_k[i].reshape(DB, WINDOW, D_KV)
    cv = cache_v[i].reshape(DB, WINDOW, D_KV)
    x1s, nks, nvs, ncs = _mixer_sample(x_sample.reshape(DB, D), ck, cv, state_conv[i], sinks, nmix, win, qg, kg,
                                       dww, dwb, lng, lnb, wout, _tile_for(DB, 32))
    ys = tail(x1s, p_sample[i].reshape(DB, -1)).reshape(DB, 1, D)

    kv5 = lambda a: a.reshape(1, a.shape[0], WINDOW, N_KV_HEADS, HEAD_DIM)
    return (yp, ys, kv5(nkp), kv5(nvp), ncp[None], kv5(nks), kv5(nvs), ncs[None])
```
